```python
import math
import jax
import jax.numpy as jnp
from jax import lax
import numpy as np

D_MODEL = 2048
BATCH = 4
SEQ = 4096
DEPTH = 4

D_MIX = D_MODEL
A_WIDTH = D_MIX // 4
A_GROUPS = 4
A_GROUP_DIM = A_WIDTH // A_GROUPS
SGU_CHUNK = 128
R_WIDTH = D_MIX // 4
R_HEADS = 4
R_HEAD_DIM = R_WIDTH // R_HEADS
RET_CHUNK = 128
C_WIDTH = D_MIX // 2
MLA_HEADS = 8
MLA_V_DIM = C_WIDTH // MLA_HEADS
MLA_NOPE = 128
MLA_ROPE = 64
Q_LORA = D_MODEL // 4
KV_LORA = D_MODEL // 8
Q_BLOCK = 128
MEM_LEN = 256
XA_HEADS = 4
XA_HEAD_DIM = D_MODEL // XA_HEADS
D_FF = ((8 * D_MODEL // 3 + 255) // 256) * 256
ROPE_BASE = 10000.0
EPS = 1e-6
IN_SIZES = (A_WIDTH, A_WIDTH,
            R_WIDTH, R_WIDTH, R_WIDTH, R_WIDTH,
            Q_LORA, KV_LORA, MLA_ROPE)
IN_COLS = sum(IN_SIZES)

kernel_name = 'hybrid_gmlp_retention_mla_macaron'


def _split_points():
    pts, acc = [], 0
    for s in IN_SIZES[:-1]:
        acc += s
        pts.append(acc)
    return pts


def rmsnorm(x, g):
    xf = x.astype(jnp.float32)
    y = xf * lax.rsqrt(jnp.mean(xf * xf, axis=-1, keepdims=True) + EPS)
    return (y * g.astype(jnp.float32)).astype(x.dtype)


def swiglu(x, w_gate, w_up, w_down):
    return (jax.nn.silu(x @ w_gate) * (x @ w_up)) @ w_down


def rope(x):
    S, dim = x.shape[1], x.shape[-1]
    half = dim // 2
    pos = jnp.arange(S, dtype=jnp.float32)
    inv_freq = ROPE_BASE ** (-jnp.arange(half, dtype=jnp.float32) * 2.0 / dim)
    ang = pos[:, None] * inv_freq[None, :]
    shape = (1, S) + (1,) * (x.ndim - 3) + (half,)
    cos = jnp.cos(ang).reshape(shape)
    sin = jnp.sin(ang).reshape(shape)
    xf = x.astype(jnp.float32)
    x1, x2 = xf[..., :half], xf[..., half:]
    return jnp.concatenate([x1 * cos - x2 * sin, x2 * cos + x1 * sin], axis=-1).astype(x.dtype)


def sgu_mixer(u, v, norm_g, w_s, b):
    B, S, _ = u.shape
    N = S // SGU_CHUNK
    u = jax.nn.gelu(u)
    v = rmsnorm(jax.nn.gelu(v), norm_g)
    v = v.reshape(B, N, SGU_CHUNK, A_GROUPS, A_GROUP_DIM)
    w = w_s * jnp.tril(jnp.ones((SGU_CHUNK, SGU_CHUNK), w_s.dtype))[None]
    z = jnp.einsum('gts,bnsgc->bntgc', w, v) + b.T[:, :, None]
    return u * z.reshape(B, S, A_WIDTH)


def retention_chunkwise(q, k, v):
    B, S, H, dk = q.shape
    dv = v.shape[-1]
    C = RET_CHUNK
    N = S // C
    log_g = jnp.log1p(-jnp.exp2(-5.0 - jnp.arange(H, dtype=jnp.float32)))
    i = jnp.arange(C, dtype=jnp.float32)
    diff = i[:, None] - i[None, :]
    dmask = jnp.where(diff[None] >= 0,
                      jnp.exp(jnp.maximum(diff, 0.0)[None] * log_g[:, None, None]),
                      0.0).astype(q.dtype)
    zeta = jnp.exp((C - 1 - i)[None, :] * log_g[:, None]).T.astype(q.dtype)
    xi = jnp.exp((i + 1)[None, :] * log_g[:, None]).T.astype(q.dtype)
    g_chunk = jnp.exp(C * log_g).astype(q.dtype)

    q = q.reshape(B, N, C, H, dk)
    k = k.reshape(B, N, C, H, dk) * (dk ** -0.5)
    v = v.reshape(B, N, C, H, dv)

    scores = jnp.einsum('bnihd,bnjhd->bnhij', q, k) * dmask
    inner = jnp.einsum('bnhij,bnjhe->bnihe', scores, v)

    kv = jnp.einsum('bnjhd,bnjhe->nbhde', k * zeta[:, :, None], v)

    def step(state, kv_n):
        return g_chunk[None, :, None, None] * state + kv_n, state

    _, prev = lax.scan(step, jnp.zeros((B, H, dk, dv), kv.dtype), kv)
    cross = jnp.einsum('bnihd,nbhde->bnihe', q * xi[:, :, None], prev)
    return (inner + cross).reshape(B, S, H, dv)


def retention_mixer(q, k, v, g, gn):
    B, S, _ = q.shape
    q = rope(q.reshape(B, S, R_HEADS, R_HEAD_DIM))
    k = rope(k.reshape(B, S, R_HEADS, R_HEAD_DIM))
    v = v.reshape(B, S, R_HEADS, R_HEAD_DIM)
    y = retention_chunkwise(q, k, v).astype(jnp.float32)
    mu = jnp.mean(y, axis=-1, keepdims=True)
    var = jnp.mean(jnp.square(y - mu), axis=-1, keepdims=True)
    y = ((y - mu) * lax.rsqrt(var + EPS)).reshape(B, S, R_WIDTH) * gn.astype(jnp.float32)
    return (jax.nn.silu(g.astype(jnp.float32)) * y).astype(g.dtype)


def causal_mla_attention(q_nope, q_rope, k_nope, k_rope, v):
    B, S, H, _ = q_nope.shape
    nb = S // Q_BLOCK
    scale = (MLA_NOPE + MLA_ROPE) ** -0.5
    qn_b = q_nope.reshape(B, nb, Q_BLOCK, H, -1).transpose(1, 0, 2, 3, 4)
    qr_b = q_rope.reshape(B, nb, Q_BLOCK, H, -1).transpose(1, 0, 2, 3, 4)
    kpos = jnp.arange(S)

    def one_block(args):
        qn, qr, bi = args
        qpos = bi * Q_BLOCK + jnp.arange(Q_BLOCK)
        s = (jnp.einsum('bqhd,bkhd->bhqk', qn, k_nope)
             + jnp.einsum('bqhr,bkr->bhqk', qr, k_rope)).astype(jnp.float32) * scale
        s = jnp.where(kpos[None, :] <= qpos[:, None], s, jnp.float32(-1e30))
        p = jax.nn.softmax(s, axis=-1).astype(v.dtype)
        return jnp.einsum('bhqk,bkhd->bqhd', p, v)

    out = lax.map(one_block, (qn_b, qr_b, jnp.arange(nb)))
    return out.transpose(1, 0, 2, 3, 4).reshape(B, S, H, -1)


def mla_mixer(c_q, c_kv, k_rope, q_norm, w_uq, kv_norm, w_ukv):
    B, S, _ = c_q.shape
    q = (rmsnorm(c_q, q_norm) @ w_uq).reshape(B, S, MLA_HEADS, MLA_NOPE + MLA_ROPE)
    q_nope, q_rope = q[..., :MLA_NOPE], rope(q[..., MLA_NOPE:])
    kv = (rmsnorm(c_kv, kv_norm) @ w_ukv).reshape(B, S, MLA_HEADS, MLA_NOPE + MLA_V_DIM)
    k_nope, v = kv[..., :MLA_NOPE], kv[..., MLA_NOPE:]
    o = causal_mla_attention(q_nope, q_rope, k_nope, rope(k_rope), v)
    return o.reshape(B, S, C_WIDTH)


def memory_cross_attention(n, m, wq, wkv, wo):
    B, S, _ = n.shape
    M = m.shape[1]
    q = (n @ wq).reshape(B, S, XA_HEADS, XA_HEAD_DIM)
    kv = (m @ wkv).reshape(B, M, 2, XA_HEADS, XA_HEAD_DIM)
    s = jnp.einsum('bshd,bmhd->bhsm', q, kv[:, :, 0]).astype(jnp.float32) * (XA_HEAD_DIM ** -0.5)
    p = jax.nn.softmax(s, axis=-1).astype(n.dtype)
    o = jnp.einsum('bhsm,bmhd->bshd', p, kv[:, :, 1]).reshape(B, S, D_MODEL)
    return o @ wo


def setup_inputs(seed: int = 0) -> dict:
    key = jax.random.key(seed)
    ks = jax.random.split(key, 27)
    f32 = jnp.float32
    L = DEPTH

    def w(k, shape, fan_in):
        return jax.random.normal(k, shape, f32) * (fan_in ** -0.5)

    def gain(k, shape):
        return 1.0 + 0.02 * jax.random.normal(k, shape, f32)

    return {
        'x': jax.random.normal(ks[0], (BATCH, SEQ, D_MODEL), f32),
        'mem': jax.random.normal(ks[1], (BATCH, MEM_LEN, D_MODEL), f32),
        'ffn1_norm': gain(ks[2], (L, D_MODEL)),
        'ffn1_w_gate': w(ks[3], (L, D_MODEL, D_FF), D_MODEL),
        'ffn1_w_up': w(ks[4], (L, D_MODEL, D_FF), D_MODEL),
        'ffn1_w_down': w(ks[5], (L, D_FF, D_MODEL), D_FF),
        'mix_norm': gain(ks[6], (L, D_MODEL)),
        'w_in': w(ks[7], (L, D_MODEL, IN_COLS), D_MODEL),
        'sgu_norm': gain(ks[8], (L, A_WIDTH)),
        'sgu_w_s': w(ks[9], (L, A_GROUPS, SGU_CHUNK, SGU_CHUNK), SGU_CHUNK),
        'sgu_b': gain(ks[10], (L, A_GROUPS, SGU_CHUNK)),
        'ret_gn': gain(ks[11], (L, R_WIDTH)),
        'q_norm': gain(ks[12], (L, Q_LORA)),
        'w_uq': w(ks[13], (L, Q_LORA, MLA_HEADS * (MLA_NOPE + MLA_ROPE)), Q_LORA),
        'kv_norm': gain(ks[14], (L, KV_LORA)),
        'w_ukv': w(ks[15], (L, KV_LORA, MLA_HEADS * (MLA_NOPE + MLA_V_DIM)), KV_LORA),
        'w_out': w(ks[16], (L, D_MIX, D_MODEL), D_MIX),
        'xa_norm': gain(ks[17], (L, D_MODEL)),
        'mem_norm': gain(ks[18], (L, D_MODEL)),
        'xa_wq': w(ks[19], (L, D_MODEL, D_MODEL), D_MODEL),
        'xa_wkv': w(ks[20], (L, D_MODEL, 2 * D_MODEL), D_MODEL),
        'xa_wo': w(ks[21], (L, D_MODEL, D_MODEL), D_MODEL),
        'ffn2_norm': gain(ks[22], (L, D_MODEL)),
        'ffn2_w_gate': w(ks[23], (L, D_MODEL, D_FF), D_MODEL),
        'ffn2_w_up': w(ks[24], (L, D_MODEL, D_FF), D_MODEL),
        'ffn2_w_down': w(ks[25], (L, D_FF, D_MODEL), D_FF),
        'final_norm': gain(ks[26], (D_MODEL,)),
    }


def reference(x, mem, ffn1_norm, ffn1_w_gate, ffn1_w_up, ffn1_w_down, mix_norm, w_in,
              sgu_norm, sgu_w_s, sgu_b, ret_gn, q_norm, w_uq, kv_norm, w_ukv, w_out,
              xa_norm, mem_norm, xa_wq, xa_wkv, xa_wo, ffn2_norm, ffn2_w_gate, ffn2_w_up,
              ffn2_w_down, final_norm):
    pts = _split_points()
    h = x
    for l in range(DEPTH):
        h = h + 0.5 * swiglu(rmsnorm(h, ffn1_norm[l]), ffn1_w_gate[l], ffn1_w_up[l], ffn1_w_down[l])
        n = rmsnorm(h, mix_norm[l])
        proj = n @ w_in[l]
        a_u, a_v, r_q, r_k, r_v, r_g, c_q, c_kv, c_kr = jnp.split(proj, pts, axis=-1)
        y_a = sgu_mixer(a_u, a_v, sgu_norm[l], sgu_w_s[l], sgu_b[l])
        y_r = retention_mixer(r_q, r_k, r_v, r_g, ret_gn[l])
        y_c = mla_mixer(c_q, c_kv, c_kr, q_norm[l], w_uq[l], kv_norm[l], w_ukv[l])
        h = h + jnp.concatenate([y_a, y_r, y_c], axis=-1) @ w_out[l]
        h = h + memory_cross_attention(rmsnorm(h, xa_norm[l]), rmsnorm(mem, mem_norm[l]),
                                       xa_wq[l], xa_wkv[l], xa_wo[l])
        h = h + 0.5 * swiglu(rmsnorm(h, ffn2_norm[l]), ffn2_w_gate[l], ffn2_w_up[l], ffn2_w_down[l])
    return rmsnorm(h, final_norm)
```

```python
import functools
import math

import jax
import jax.numpy as jnp
from jax import lax
from jax.experimental import pallas as pl
from jax.experimental.pallas import tpu as pltpu

F32 = jnp.float32
BF16 = jnp.bfloat16

EPS = 1e-6
ROPE_BASE = 10000.0
CHUNK = 128
A_GROUPS = 4
R_HEADS = 4
MLA_HEADS = 8
MLA_NOPE = 128
MLA_ROPE = 64
MLA_V = 128
XA_HEADS = 4
LANES = 128

VMEM_LIMIT = 56 * 1024 * 1024


def _cparams(sem):
    return pltpu.CompilerParams(dimension_semantics=sem, vmem_limit_bytes=VMEM_LIMIT)


def _tile(n, pref):
    t = min(n, pref)
    assert n % t == 0, (n, t)
    return t


def _rms(x, g):
    return x * lax.rsqrt(jnp.mean(x * x, axis=-1, keepdims=True) + EPS) * g


def _silu(x):
    return x * (1.0 / (1.0 + jnp.exp(-x)))


def _dot(a, b):
    return jnp.dot(a, b, preferred_element_type=F32)


def _dot_nt(a, b):
    return lax.dot_general(a, b, (((1,), (1,)), ((), ())), preferred_element_type=F32)


def _ffn_kernel(h_ref, g_ref, wg_ref, wu_ref, wd_ref, fg_ref, o_ref, n_ref, acc_ref, *, nf, final):
    f = pl.program_id(1)

    @pl.when(f == 0)
    def _():
        n_ref[...] = _rms(h_ref[...], g_ref[...]).astype(BF16)
        acc_ref[...] = jnp.zeros_like(acc_ref)

    n = n_ref[...]
    a = _dot(n, wg_ref[...])
    b = _dot(n, wu_ref[...])
    acc_ref[...] += _dot((_silu(a) * b).astype(BF16), wd_ref[...])

    @pl.when(f == nf - 1)
    def _():
        y = h_ref[...] + 0.5 * acc_ref[...]
        if final:
            y = _rms(y, fg_ref[...])
        o_ref[...] = y


def _ffn(h, g, wg, wu, wd, l, final_g, final):
    T, D = h.shape
    F = wg.shape[-1]
    tm = _tile(T, 512)
    tf = _tile(F, 512)
    nf = F // tf
    return pl.pallas_call(
        functools.partial(_ffn_kernel, nf=nf, final=final),
        grid=(T // tm, nf),
        in_specs=[
            pl.BlockSpec((tm, D), lambda i, f: (i, 0)),
            pl.BlockSpec((None, 1, D), lambda i, f: (l, 0, 0)),
            pl.BlockSpec((None, D, tf), lambda i, f: (l, 0, f)),
            pl.BlockSpec((None, D, tf), lambda i, f: (l, 0, f)),
            pl.BlockSpec((None, tf, D), lambda i, f: (l, f, 0)),
            pl.BlockSpec((1, D), lambda i, f: (0, 0)),
        ],
        out_specs=pl.BlockSpec((tm, D), lambda i, f: (i, 0)),
        out_shape=jax.ShapeDtypeStruct((T, D), F32),
        scratch_shapes=[pltpu.VMEM((tm, D), BF16), pltpu.VMEM((tm, D), F32)],
        compiler_params=_cparams(("parallel", "arbitrary")),
        name="ffn",
    )(h, g, wg, wu, wd, final_g)


def _norm_mm_kernel(x_ref, g_ref, w_ref, o_ref, n_ref):
    @pl.when(pl.program_id(1) == 0)
    def _():
        n_ref[...] = _rms(x_ref[...], g_ref[...]).astype(BF16)

    o_ref[...] = _dot(n_ref[...], w_ref[...]).astype(o_ref.dtype)


def _norm_mm(x, g, w, l, out_dtype, tm_pref=512, tn_pref=1024):
    M, K = x.shape
    N = w.shape[-1]
    tm = _tile(M, tm_pref)
    tn = _tile(N, tn_pref)
    return pl.pallas_call(
        _norm_mm_kernel,
        grid=(M // tm, N // tn),
        in_specs=[
            pl.BlockSpec((tm, K), lambda i, j: (i, 0)),
            pl.BlockSpec((None, 1, K), lambda i, j: (l, 0, 0)),
            pl.BlockSpec((None, K, tn), lambda i, j: (l, 0, j)),
        ],
        out_specs=pl.BlockSpec((tm, tn), lambda i, j: (i, j)),
        out_shape=jax.ShapeDtypeStruct((M, N), out_dtype),
        scratch_shapes=[pltpu.VMEM((tm, K), BF16)],
        compiler_params=_cparams(("parallel", "arbitrary")),
        name="norm_mm",
    )(x, g, w)


def _mm_res_kernel(*refs, ny):
    ys, ws, h_ref, o_ref = refs[:ny], refs[ny:2 * ny], refs[2 * ny], refs[2 * ny + 1]
    acc = h_ref[...]
    for y_ref, w_ref in zip(ys, ws):
        acc = acc + _dot(y_ref[...], w_ref[...])
    o_ref[...] = acc


def _mm_res(ys, w, l, h, tm_pref=512, tn_pref=1024):
    T, N = h.shape
    tm = _tile(T, tm_pref)
    tn = _tile(N, tn_pref)
    y_specs, w_specs, row = [], [], 0
    for y in ys:
        k = y.shape[1]
        assert row % k == 0
        y_specs.append(pl.BlockSpec((tm, k), lambda i, j: (i, 0)))
        w_specs.append(pl.BlockSpec((None, k, tn), functools.partial(lambda i, j, rb: (l, rb, j), rb=row // k)))
        row += k
    assert row == w.shape[1]
    return pl.pallas_call(
        functools.partial(_mm_res_kernel, ny=len(ys)),
        grid=(T // tm, N // tn),
        in_specs=y_specs + w_specs + [pl.BlockSpec((tm, tn), lambda i, j: (i, j))],
        out_specs=pl.BlockSpec((tm, tn), lambda i, j: (i, j)),
        out_shape=jax.ShapeDtypeStruct((T, N), F32),
        compiler_params=_cparams(("parallel", "arbitrary")),
        name="mm_res",
    )(*ys, *([w] * len(ys)), h)


def _sgu_kernel(u_ref, v_ref, g_ref, ws_ref, bt_ref, o_ref, *, nchunk):
    C = CHUNK
    row = lax.broadcasted_iota(jnp.int32, (C, C), 0)
    col = lax.broadcasted_iota(jnp.int32, (C, C), 1)
    wm = [jnp.where(row >= col, ws_ref[g], 0.0).astype(BF16) for g in range(A_GROUPS)]
    bt = bt_ref[...]
    for c in range(nchunk):
        rows = slice(c * C, (c + 1) * C)
        u = jax.nn.gelu(u_ref[rows, :])
        v = _rms(jax.nn.gelu(v_ref[rows, :]), g_ref[...]).astype(BF16)
        for g in range(A_GROUPS):
            cols = slice(g * C, (g + 1) * C)
            z = _dot(wm[g], v[:, cols]) + bt[:, g:g + 1]
            o_ref[rows, cols] = (u[:, cols] * z).astype(o_ref.dtype)


def _sgu(proj, g, ws, bt, l, width):
    T = proj.shape[0]
    ts = _tile(T, 4 * CHUNK)
    return pl.pallas_call(
        functools.partial(_sgu_kernel, nchunk=ts // CHUNK),
        grid=(T // ts,),
        in_specs=[
            pl.BlockSpec((ts, width), lambda i: (i, 0)),
            pl.BlockSpec((ts, width), lambda i: (i, 1)),
            pl.BlockSpec((None, 1, width), lambda i: (l, 0, 0)),
            pl.BlockSpec((None, A_GROUPS, CHUNK, CHUNK), lambda i: (l, 0, 0, 0)),
            pl.BlockSpec((None, CHUNK, A_GROUPS), lambda i: (l, 0, 0)),
        ],
        out_specs=pl.BlockSpec((ts, width), lambda i: (i, 0)),
        out_shape=jax.ShapeDtypeStruct((T, width), BF16),
        compiler_params=_cparams(("parallel",)),
        name="sgu",
    )(proj, proj, g, ws, bt)


def _ret_kernel(q_ref, k_ref, v_ref, gate_ref, cos_ref, sin_ref, dmask_ref, zeta_ref, xi_ref, gch_ref, gn_ref,
                o_ref, state_ref, *, nchunk):
    C = CHUNK
    hd = C
    half = hd // 2

    @pl.when(pl.program_id(1) == 0)
    def _():
        state_ref[...] = jnp.zeros_like(state_ref)

    for c in range(nchunk):
        rows = slice(c * C, (c + 1) * C)
        cos = cos_ref[rows, :]
        sin = sin_ref[rows, :]
        for h in range(R_HEADS):
            cols = slice(h * hd, (h + 1) * hd)
            q = q_ref[rows, cols]
            k = k_ref[rows, cols]
            q = q * cos + pltpu.roll(q, half, 1) * sin
            k = (k * cos + pltpu.roll(k, half, 1) * sin) * (hd ** -0.5)
            vb = v_ref[rows, cols].astype(BF16)
            scores = _dot_nt(q.astype(BF16), k.astype(BF16)) * dmask_ref[h]
            state = state_ref[h]
            y = _dot(scores.astype(BF16), vb) + _dot((q * xi_ref[:, cols]).astype(BF16), state.astype(BF16))
            kz = (k * zeta_ref[:, cols]).astype(BF16)
            state_ref[h] = gch_ref[h] * state + _dot(kz.T, vb)
            mu = jnp.mean(y, axis=-1, keepdims=True)
            yc = y - mu
            var = jnp.mean(yc * yc, axis=-1, keepdims=True)
            yn = yc * lax.rsqrt(var + EPS) * gn_ref[:, cols]
            o_ref[rows, cols] = (_silu(gate_ref[rows, cols]) * yn).astype(o_ref.dtype)


def _retention(proj, cos, sin, dmask, zeta, xi, gch, gn, l, B, S, width, col0):
    T = proj.shape[0]
    tr = _tile(S, 4 * CHUNK)
    ns = S // tr
    cb = col0 // width
    assert col0 % width == 0
    tok = lambda b, n: b * ns + n
    return pl.pallas_call(
        functools.partial(_ret_kernel, nchunk=tr // CHUNK),
        grid=(B, ns),
        in_specs=[
            pl.BlockSpec((tr, width), lambda b, n: (tok(b, n), cb)),
            pl.BlockSpec((tr, width), lambda b, n: (tok(b, n), cb + 1)),
            pl.BlockSpec((tr, width), lambda b, n: (tok(b, n), cb + 2)),
            pl.BlockSpec((tr, width), lambda b, n: (tok(b, n), cb + 3)),
            pl.BlockSpec((tr, CHUNK), lambda b, n: (n, 0)),
            pl.BlockSpec((tr, CHUNK), lambda b, n: (n, 0)),
            pl.BlockSpec((R_HEADS, CHUNK, CHUNK), lambda b, n: (0, 0, 0)),
            pl.BlockSpec((CHUNK, width), lambda b, n: (0, 0)),
            pl.BlockSpec((CHUNK, width), lambda b, n: (0, 0)),
            pl.BlockSpec((R_HEADS, 1, CHUNK), lambda b, n: (0, 0, 0)),
            pl.BlockSpec((None, 1, width), lambda b, n: (l, 0, 0)),
        ],
        out_specs=pl.BlockSpec((tr, width), lambda b, n: (tok(b, n), 0)),
        out_shape=jax.ShapeDtypeStruct((T, width), BF16),
        scratch_shapes=[pltpu.VMEM((R_HEADS, CHUNK, CHUNK), F32)],
        compiler_params=_cparams(("parallel", "arbitrary")),
        name="retention",
    )(proj, proj, proj, proj, cos, sin, dmask, zeta, xi, gch, gn)


def _retention_tables(S):
    C, H, dim = CHUNK, R_HEADS, CHUNK
    half = dim // 2
    pos = jnp.arange(S, dtype=F32)
    inv_freq = ROPE_BASE ** (-jnp.arange(half, dtype=F32) * 2.0 / dim)
    ang = pos[:, None] * inv_freq[None, :]
    cos, sin = jnp.cos(ang), jnp.sin(ang)
    cos_t = jnp.concatenate([cos, cos], axis=-1)
    sin_t = jnp.concatenate([-sin, sin], axis=-1)
    log_g = jnp.log1p(-jnp.exp2(-5.0 - jnp.arange(H, dtype=F32)))
    i = jnp.arange(C, dtype=F32)
    diff = i[:, None] - i[None, :]
    dmask = jnp.where(diff[None] >= 0, jnp.exp(jnp.maximum(diff, 0.0)[None] * log_g[:, None, None]), 0.0)
    zeta = jnp.exp((C - 1 - i)[None, :] * log_g[:, None]).T
    xi = jnp.exp((i + 1)[None, :] * log_g[:, None]).T
    gch = jnp.exp(C * log_g)
    rep = lambda t: jnp.repeat(t, dim, axis=1)
    return cos_t, sin_t, dmask, rep(zeta), rep(xi), jnp.broadcast_to(gch[:, None, None], (H, 1, C))


def _mla_proj_kernel(cq_ref, ckv_ref, kr_ref, qg_ref, kvg_ref, wq_ref, wkv_ref, rt_ref, q_ref, k_ref, v_ref):
    hq = MLA_NOPE + 2 * MLA_ROPE
    scale = (MLA_NOPE + MLA_ROPE) ** -0.5
    lane = lax.broadcasted_iota(jnp.int32, (1, 2 * MLA_ROPE), 1)
    keep = jnp.where(lane < MLA_ROPE, scale, 0.0)
    rt = rt_ref[...]

    def rope(y):
        z = y * rt
        return z + pltpu.roll(z, MLA_ROPE, 1)

    kr = rope(kr_ref[...]).astype(BF16)
    q = _dot(_rms(cq_ref[...], qg_ref[...]).astype(BF16), wq_ref[...])
    kv = _dot(_rms(ckv_ref[...], kvg_ref[...]).astype(BF16), wkv_ref[...])
    for h in range(MLA_HEADS):
        q_ref[:, h * hq:h * hq + MLA_NOPE] = (q[:, h * hq:h * hq + MLA_NOPE] * scale).astype(BF16)
        q_ref[:, h * hq + MLA_NOPE:(h + 1) * hq] = (rope(q[:, h * hq + MLA_NOPE:(h + 1) * hq]) * keep).astype(BF16)
        k_ref[:, h * hq:h * hq + MLA_NOPE] = kv[:, h * MLA_NOPE:(h + 1) * MLA_NOPE].astype(BF16)
        k_ref[:, h * hq + MLA_NOPE:(h + 1) * hq] = kr
    v_ref[...] = kv[:, MLA_HEADS * MLA_NOPE:].astype(BF16)


def _mla_proj(proj, qg, kvg, wq, wkv, rt, l, S, q_lora, kv_lora, col0):
    T = proj.shape[0]
    tm = _tile(S, 512)
    ns = S // tm
    hq = MLA_NOPE + 2 * MLA_ROPE
    c_kv0 = col0 + q_lora
    c_kr0 = c_kv0 + kv_lora
    assert col0 % q_lora == 0 and c_kv0 % kv_lora == 0 and c_kr0 % LANES == 0
    return pl.pallas_call(
        _mla_proj_kernel,
        grid=(T // tm,),
        in_specs=[
            pl.BlockSpec((tm, q_lora), lambda i: (i, col0 // q_lora)),
            pl.BlockSpec((tm, kv_lora), lambda i: (i, c_kv0 // kv_lora)),
            pl.BlockSpec((tm, 2 * MLA_ROPE), lambda i: (i, c_kr0 // LANES)),
            pl.BlockSpec((None, 1, q_lora), lambda i: (l, 0, 0)),
            pl.BlockSpec((None, 1, kv_lora), lambda i: (l, 0, 0)),
            pl.BlockSpec((None, q_lora, MLA_HEADS * hq), lambda i: (l, 0, 0)),
            pl.BlockSpec((None, kv_lora, MLA_HEADS * (MLA_NOPE + MLA_V)), lambda i: (l, 0, 0)),
            pl.BlockSpec((tm, 2 * MLA_ROPE), lambda i: (i % ns, 0)),
        ],
        out_specs=[
            pl.BlockSpec((tm, MLA_HEADS * hq), lambda i: (i, 0)),
            pl.BlockSpec((tm, MLA_HEADS * hq), lambda i: (i, 0)),
            pl.BlockSpec((tm, MLA_HEADS * MLA_V), lambda i: (i, 0)),
        ],
        out_shape=[
            jax.ShapeDtypeStruct((T, MLA_HEADS * hq), BF16),
            jax.ShapeDtypeStruct((T, MLA_HEADS * hq), BF16),
            jax.ShapeDtypeStruct((T, MLA_HEADS * MLA_V), BF16),
        ],
        compiler_params=_cparams(("parallel",)),
        name="mla_proj",
    )(proj, proj, proj, qg, kvg, wq, wkv, rt)


def _flash_kernel(q_ref, k_ref, v_ref, o_ref, *, tq):
    qi = pl.program_id(2)
    q = q_ref[...]

    def step(j, carry, masked):
        m, l, acc = carry
        start = pl.multiple_of(j * tq, tq)
        s = _dot_nt(q, k_ref[pl.ds(start, tq), :])
        if masked:
            row = lax.broadcasted_iota(jnp.int32, s.shape, 0)
            col = lax.broadcasted_iota(jnp.int32, s.shape, 1)
            s = jnp.where(col <= row, s, -1e30)
        m_new = jnp.maximum(m, jnp.max(s, axis=-1, keepdims=True))
        p = jnp.exp(s - m_new)
        alpha = jnp.exp(m - m_new)
        l = alpha * l + jnp.sum(p, axis=-1, keepdims=True)
        acc = alpha * acc + _dot(p.astype(BF16), v_ref[pl.ds(start, tq), :])
        return m_new, l, acc

    init = (jnp.full((tq, 1), -1e30, F32), jnp.zeros((tq, 1), F32), jnp.zeros((tq, MLA_V), F32))
    carry = lax.fori_loop(0, qi, functools.partial(step, masked=False), init)
    _, l, acc = step(qi, carry, True)
    o_ref[...] = (acc / l).astype(o_ref.dtype)


def _flash(q, k, v, B, S):
    T = q.shape[0]
    tq = _tile(S, 512)
    nq = S // tq
    hq = MLA_NOPE + 2 * MLA_ROPE
    return pl.pallas_call(
        functools.partial(_flash_kernel, tq=tq),
        grid=(B, MLA_HEADS, nq),
        in_specs=[
            pl.BlockSpec((tq, hq), lambda b, h, i: (b * nq + i, h)),
            pl.BlockSpec((S, hq), lambda b, h, i: (b, h)),
            pl.BlockSpec((S, MLA_V), lambda b, h, i: (b, h)),
        ],
        out_specs=pl.BlockSpec((tq, MLA_V), lambda b, h, i: (b * nq + i, h)),
        out_shape=jax.ShapeDtypeStruct((T, MLA_HEADS * MLA_V), BF16),
        compiler_params=_cparams(("parallel", "parallel", "arbitrary")),
        name="mla_flash",
    )(q, k, v)


def _mla_rope_table(S):
    half = MLA_ROPE // 2
    pos = jnp.arange(S, dtype=F32)
    inv_freq = ROPE_BASE ** (-jnp.arange(half, dtype=F32) * 2.0 / MLA_ROPE)
    ang = pos[:, None] * inv_freq[None, :]
    cos, sin = jnp.cos(ang), jnp.sin(ang)
    return jnp.concatenate([cos, cos, -sin, sin], axis=-1)


def _swap_halves(w):
    half = w.shape[-1] // 2
    return jnp.concatenate([w[..., half:], w[..., :half]], axis=-1)


def _xattn_kernel(q_ref, k_ref, v_ref, o_ref, *, hd):
    scale = hd ** -0.5
    for h in range(XA_HEADS):
        cols = slice(h * hd, (h + 1) * hd)
        s = _dot_nt(q_ref[:, cols], k_ref[:, cols]) * scale
        e = jnp.exp(s - jnp.max(s, axis=-1, keepdims=True))
        p = e / jnp.sum(e, axis=-1, keepdims=True)
        o_ref[:, cols] = _dot(p.astype(BF16), v_ref[:, cols]).astype(o_ref.dtype)


def _xattn(q, kv, S, M):
    T, D = q.shape
    tm = _tile(S, 512)
    ns = S // tm
    return pl.pallas_call(
        functools.partial(_xattn_kernel, hd=D // XA_HEADS),
        grid=(T // tm,),
        in_specs=[
            pl.BlockSpec((tm, D), lambda i: (i, 0)),
            pl.BlockSpec((M, D), lambda i: (i // ns, 0)),
            pl.BlockSpec((M, D), lambda i: (i // ns, 1)),
        ],
        out_specs=pl.BlockSpec((tm, D), lambda i: (i, 0)),
        out_shape=jax.ShapeDtypeStruct((T, D), BF16),
        compiler_params=_cparams(("parallel",)),
        name="xattn",
    )(q, kv, kv)


def kernel(x, mem, ffn1_norm, ffn1_w_gate, ffn1_w_up, ffn1_w_down, mix_norm, w_in, sgu_norm, sgu_w_s, sgu_b, ret_gn, q_norm, w_uq, kv_norm, w_ukv, w_out, xa_norm, mem_norm, xa_wq, xa_wkv, xa_wo, ffn2_norm, ffn2_w_gate, ffn2_w_up, ffn2_w_down, final_norm):
    B, S, D = x.shape
    M = mem.shape[1]
    L = w_in.shape[0]
    T = B * S
    a_width = sgu_norm.shape[-1]
    r_width = ret_gn.shape[-1]
    q_lora = q_norm.shape[-1]
    kv_lora = kv_norm.shape[-1]
    assert a_width == A_GROUPS * CHUNK and r_width == R_HEADS * CHUNK and S % CHUNK == 0
    ret_col0 = 2 * a_width
    mla_col0 = ret_col0 + 4 * r_width
    kr_col0 = mla_col0 + q_lora + kv_lora
    assert w_in.shape[-1] == kr_col0 + MLA_ROPE

    bf = lambda w: w.astype(BF16)
    row = lambda g: g.reshape(g.shape[0], 1, g.shape[1])

    n_in = kr_col0 + 2 * MLA_ROPE
    n_in_pad = -(-n_in // 1024) * 1024
    w_in_x = jnp.concatenate(
        [bf(w_in), bf(_swap_halves(w_in[..., kr_col0:])), jnp.zeros((L, D, n_in_pad - n_in), BF16)], axis=-1)
    wq4 = w_uq.reshape(L, q_lora, MLA_HEADS, MLA_NOPE + MLA_ROPE)
    wq_x = bf(jnp.concatenate([wq4, _swap_halves(wq4[..., MLA_NOPE:])], axis=-1)).reshape(L, q_lora, -1)
    wkv4 = w_ukv.reshape(L, kv_lora, MLA_HEADS, MLA_NOPE + MLA_V)
    wkv_x = bf(jnp.concatenate([wkv4[..., :MLA_NOPE].reshape(L, kv_lora, -1),
                                wkv4[..., MLA_NOPE:].reshape(L, kv_lora, -1)], axis=-1))
    sgu_bt = jnp.swapaxes(sgu_b, 1, 2)
    ret_tabs = _retention_tables(S)
    mla_rt = _mla_rope_table(S)
    w1g, w1u, w1d = bf(ffn1_w_gate), bf(ffn1_w_up), bf(ffn1_w_down)
    w2g, w2u, w2d = bf(ffn2_w_gate), bf(ffn2_w_up), bf(ffn2_w_down)
    w_out_b, xa_wq_b, xa_wkv_b, xa_wo_b = bf(w_out), bf(xa_wq), bf(xa_wkv), bf(xa_wo)
    fin = final_norm.reshape(1, D)

    h = x.reshape(T, D)
    mem2 = mem.reshape(B * M, D)
    for l in range(L):
        h = _ffn(h, row(ffn1_norm), w1g, w1u, w1d, l, fin, False)
        proj = _norm_mm(h, row(mix_norm), w_in_x, l, F32)
        y_a = _sgu(proj, row(sgu_norm), sgu_w_s, sgu_bt, l, a_width)
        y_r = _retention(proj, *ret_tabs, row(ret_gn), l, B, S, r_width, ret_col0)
        q, k, v = _mla_proj(proj, row(q_norm), row(kv_norm), wq_x, wkv_x, mla_rt, l, S, q_lora, kv_lora, mla_col0)
        y_c = _flash(q, k, v, B, S)
        h = _mm_res([y_a, y_r, y_c], w_out_b, l, h)
        kvm = _norm_mm(mem2, row(mem_norm), xa_wkv_b, l, BF16)
        xq = _norm_mm(h, row(xa_norm), xa_wq_b, l, BF16)
        o = _xattn(xq, kvm, S, M)
        h = _mm_res([o], xa_wo_b, l, h)
        h = _ffn(h, row(ffn2_norm), w2g, w2u, w2d, l, fin, l == L - 1)
    return h.reshape(B, S, D)
```

```python
import functools
import math

import jax
import jax.numpy as jnp
from jax import lax
from jax.experimental import pallas as pl
from jax.experimental.pallas import tpu as pltpu

F32 = jnp.float32
BF16 = jnp.bfloat16

EPS = 1e-6
ROPE_BASE = 10000.0
CHUNK = 128
A_GROUPS = 4
R_HEADS = 4
MLA_HEADS = 8
MLA_NOPE = 128
MLA_ROPE = 64
MLA_V = 128
XA_HEADS = 4
LANES = 128

VMEM_LIMIT = 56 * 1024 * 1024


def _cparams(sem):
    return pltpu.CompilerParams(dimension_semantics=sem, vmem_limit_bytes=VMEM_LIMIT)


def _tile(n, pref):
    t = min(n, pref)
    assert n % t == 0, (n, t)
    return t


def _rms(x, g):
    return x * lax.rsqrt(jnp.mean(x * x, axis=-1, keepdims=True) + EPS) * g


def _silu(x):
    return x * (1.0 / (1.0 + jnp.exp(-x)))


def _dot(a, b):
    return jnp.dot(a, b, preferred_element_type=F32)


def _dot_nt(a, b):
    return lax.dot_general(a, b, (((1,), (1,)), ((), ())), preferred_element_type=F32)


def _ffn_kernel(h_ref, g_ref, wg_ref, wu_ref, wd_ref, fg_ref, o_ref, n_ref, acc_ref, *, nf, final):
    f = pl.program_id(1)

    @pl.when(f == 0)
    def _():
        n_ref[...] = _rms(h_ref[...], g_ref[...]).astype(BF16)
        acc_ref[...] = jnp.zeros_like(acc_ref)

    n = n_ref[...]
    a = _dot(n, wg_ref[...])
    b = _dot(n, wu_ref[...])
    acc_ref[...] += _dot((_silu(a) * b).astype(BF16), wd_ref[...])

    @pl.when(f == nf - 1)
    def _():
        y = h_ref[...] + 0.5 * acc_ref[...]
        if final:
            y = _rms(y, fg_ref[...])
        o_ref[...] = y


def _ffn(h, g, wg, wu, wd, l, final_g, final):
    T, D = h.shape
    F = wg.shape[-1]
    tm = _tile(T, 512)
    tf = _tile(F, 512)
    nf = F // tf
    return pl.pallas_call(
        functools.partial(_ffn_kernel, nf=nf, final=final),
        grid=(T // tm, nf),
        in_specs=[
            pl.BlockSpec((tm, D), lambda i, f: (i, 0)),
            pl.BlockSpec((None, 1, D), lambda i, f: (l, 0, 0)),
            pl.BlockSpec((None, D, tf), lambda i, f: (l, 0, f)),
            pl.BlockSpec((None, D, tf), lambda i, f: (l, 0, f)),
            pl.BlockSpec((None, tf, D), lambda i, f: (l, f, 0)),
            pl.BlockSpec((1, D), lambda i, f: (0, 0)),
        ],
        out_specs=pl.BlockSpec((tm, D), lambda i, f: (i, 0)),
        out_shape=jax.ShapeDtypeStruct((T, D), F32),
        scratch_shapes=[pltpu.VMEM((tm, D), BF16), pltpu.VMEM((tm, D), F32)],
        compiler_params=_cparams(("parallel", "arbitrary")),
        name="ffn",
    )(h, g, wg, wu, wd, final_g)


COL_CHUNK = 1024


def _resident(block_shape, index_map):
    return pl.BlockSpec(block_shape, index_map, pipeline_mode=pl.Buffered(1))


def _norm_mm_kernel(x_ref, g_ref, w_ref, o_ref):
    n = _rms(x_ref[...], g_ref[...]).astype(BF16)
    N = o_ref.shape[1]
    tn = min(N, COL_CHUNK)
    for c in range(N // tn):
        o_ref[:, c * tn:(c + 1) * tn] = _dot(n, w_ref[:, c * tn:(c + 1) * tn]).astype(o_ref.dtype)


def _norm_mm(x, g, w, l, out_dtype, tm_pref=512):
    M, K = x.shape
    N = w.shape[-1]
    tm = _tile(M, tm_pref)
    return pl.pallas_call(
        _norm_mm_kernel,
        grid=(M // tm,),
        in_specs=[
            pl.BlockSpec((tm, K), lambda i: (i, 0)),
            _resident((None, 1, K), lambda i: (l, 0, 0)),
            _resident((None, K, N), lambda i: (l, 0, 0)),
        ],
        out_specs=pl.BlockSpec((tm, N), lambda i: (i, 0)),
        out_shape=jax.ShapeDtypeStruct((M, N), out_dtype),
        compiler_params=_cparams(("parallel",)),
        name="norm_mm",
    )(x, g, w)


def _mm_res_kernel(*refs, ny):
    ys, ws, h_ref, o_ref = refs[:ny], refs[ny:2 * ny], refs[2 * ny], refs[2 * ny + 1]
    N = o_ref.shape[1]
    tn = min(N, COL_CHUNK)
    for c in range(N // tn):
        cols = slice(c * tn, (c + 1) * tn)
        acc = h_ref[:, cols]
        for y_ref, w_ref in zip(ys, ws):
            acc = acc + _dot(y_ref[...], w_ref[:, cols])
        o_ref[:, cols] = acc


def _mm_res(ys, w, l, h, tm_pref=512):
    T, N = h.shape
    tm = _tile(T, tm_pref)
    y_specs, w_specs, row = [], [], 0
    for y in ys:
        k = y.shape[1]
        assert row % k == 0
        y_specs.append(pl.BlockSpec((tm, k), lambda i: (i, 0)))
        w_specs.append(_resident((None, k, N), functools.partial(lambda i, rb: (l, rb, 0), rb=row // k)))
        row += k
    assert row == w.shape[1]
    return pl.pallas_call(
        functools.partial(_mm_res_kernel, ny=len(ys)),
        grid=(T // tm,),
        in_specs=y_specs + w_specs + [pl.BlockSpec((tm, N), lambda i: (i, 0))],
        out_specs=pl.BlockSpec((tm, N), lambda i: (i, 0)),
        out_shape=jax.ShapeDtypeStruct((T, N), F32),
        compiler_params=_cparams(("parallel",)),
        name="mm_res",
    )(*ys, *([w] * len(ys)), h)


def _sgu_kernel(u_ref, v_ref, g_ref, ws_ref, bt_ref, o_ref, *, nchunk):
    C = CHUNK
    row = lax.broadcasted_iota(jnp.int32, (C, C), 0)
    col = lax.broadcasted_iota(jnp.int32, (C, C), 1)
    wm = [jnp.where(row >= col, ws_ref[g], 0.0).astype(BF16) for g in range(A_GROUPS)]
    bt = bt_ref[...]
    for c in range(nchunk):
        rows = slice(c * C, (c + 1) * C)
        u = jax.nn.gelu(u_ref[rows, :])
        v = _rms(jax.nn.gelu(v_ref[rows, :]), g_ref[...]).astype(BF16)
        for g in range(A_GROUPS):
            cols = slice(g * C, (g + 1) * C)
            z = _dot(wm[g], v[:, cols]) + bt[:, g:g + 1]
            o_ref[rows, cols] = (u[:, cols] * z).astype(o_ref.dtype)


def _sgu(proj, g, ws, bt, l, width):
    T = proj.shape[0]
    ts = _tile(T, 4 * CHUNK)
    return pl.pallas_call(
        functools.partial(_sgu_kernel, nchunk=ts // CHUNK),
        grid=(T // ts,),
        in_specs=[
            pl.BlockSpec((ts, width), lambda i: (i, 0)),
            pl.BlockSpec((ts, width), lambda i: (i, 1)),
            pl.BlockSpec((None, 1, width), lambda i: (l, 0, 0)),
            pl.BlockSpec((None, A_GROUPS, CHUNK, CHUNK), lambda i: (l, 0, 0, 0)),
            pl.BlockSpec((None, CHUNK, A_GROUPS), lambda i: (l, 0, 0)),
        ],
        out_specs=pl.BlockSpec((ts, width), lambda i: (i, 0)),
        out_shape=jax.ShapeDtypeStruct((T, width), BF16),
        compiler_params=_cparams(("parallel",)),
        name="sgu",
    )(proj, proj, g, ws, bt)


def _ret_kernel(q_ref, k_ref, v_ref, gate_ref, cos_ref, sin_ref, dmask_ref, zeta_ref, xi_ref, gch_ref, gn_ref,
                o_ref, state_ref, *, nchunk):
    C = CHUNK
    hd = C
    half = hd // 2

    @pl.when(pl.program_id(1) == 0)
    def _():
        state_ref[...] = jnp.zeros_like(state_ref)

    for c in range(nchunk):
        rows = slice(c * C, (c + 1) * C)
        cos = cos_ref[rows, :]
        sin = sin_ref[rows, :]
        for h in range(R_HEADS):
            cols = slice(h * hd, (h + 1) * hd)
            q = q_ref[rows, cols]
            k = k_ref[rows, cols]
            q = q * cos + pltpu.roll(q, half, 1) * sin
            k = (k * cos + pltpu.roll(k, half, 1) * sin) * (hd ** -0.5)
            vb = v_ref[rows, cols].astype(BF16)
            scores = _dot_nt(q.astype(BF16), k.astype(BF16)) * dmask_ref[h]
            state = state_ref[h]
            y = _dot(scores.astype(BF16), vb) + _dot((q * xi_ref[:, cols]).astype(BF16), state.astype(BF16))
            kz = (k * zeta_ref[:, cols]).astype(BF16)
            state_ref[h] = gch_ref[h] * state + _dot(kz.T, vb)
            mu = jnp.mean(y, axis=-1, keepdims=True)
            yc = y - mu
            var = jnp.mean(yc * yc, axis=-1, keepdims=True)
            yn = yc * lax.rsqrt(var + EPS) * gn_ref[:, cols]
            o_ref[rows, cols] = (_silu(gate_ref[rows, cols]) * yn).astype(o_ref.dtype)


def _retention(proj, cos, sin, dmask, zeta, xi, gch, gn, l, B, S, width, col0):
    T = proj.shape[0]
    tr = _tile(S, 4 * CHUNK)
    ns = S // tr
    cb = col0 // width
    assert col0 % width == 0
    tok = lambda b, n: b * ns + n
    return pl.pallas_call(
        functools.partial(_ret_kernel, nchunk=tr // CHUNK),
        grid=(B, ns),
        in_specs=[
            pl.BlockSpec((tr, width), lambda b, n: (tok(b, n), cb)),
            pl.BlockSpec((tr, width), lambda b, n: (tok(b, n), cb + 1)),
            pl.BlockSpec((tr, width), lambda b, n: (tok(b, n), cb + 2)),
            pl.BlockSpec((tr, width), lambda b, n: (tok(b, n), cb + 3)),
            pl.BlockSpec((tr, CHUNK), lambda b, n: (n, 0)),
            pl.BlockSpec((tr, CHUNK), lambda b, n: (n, 0)),
            pl.BlockSpec((R_HEADS, CHUNK, CHUNK), lambda b, n: (0, 0, 0)),
            pl.BlockSpec((CHUNK, width), lambda b, n: (0, 0)),
            pl.BlockSpec((CHUNK, width), lambda b, n: (0, 0)),
            pl.BlockSpec((R_HEADS, 1, CHUNK), lambda b, n: (0, 0, 0)),
            pl.BlockSpec((None, 1, width), lambda b, n: (l, 0, 0)),
        ],
        out_specs=pl.BlockSpec((tr, width), lambda b, n: (tok(b, n), 0)),
        out_shape=jax.ShapeDtypeStruct((T, width), BF16),
        scratch_shapes=[pltpu.VMEM((R_HEADS, CHUNK, CHUNK), F32)],
        compiler_params=_cparams(("parallel", "arbitrary")),
        name="retention",
    )(proj, proj, proj, proj, cos, sin, dmask, zeta, xi, gch, gn)


def _retention_tables(S):
    C, H, dim = CHUNK, R_HEADS, CHUNK
    half = dim // 2
    pos = jnp.arange(S, dtype=F32)
    inv_freq = ROPE_BASE ** (-jnp.arange(half, dtype=F32) * 2.0 / dim)
    ang = pos[:, None] * inv_freq[None, :]
    cos, sin = jnp.cos(ang), jnp.sin(ang)
    cos_t = jnp.concatenate([cos, cos], axis=-1)
    sin_t = jnp.concatenate([-sin, sin], axis=-1)
    log_g = jnp.log1p(-jnp.exp2(-5.0 - jnp.arange(H, dtype=F32)))
    i = jnp.arange(C, dtype=F32)
    diff = i[:, None] - i[None, :]
    dmask = jnp.where(diff[None] >= 0, jnp.exp(jnp.maximum(diff, 0.0)[None] * log_g[:, None, None]), 0.0)
    zeta = jnp.exp((C - 1 - i)[None, :] * log_g[:, None]).T
    xi = jnp.exp((i + 1)[None, :] * log_g[:, None]).T
    gch = jnp.exp(C * log_g)
    rep = lambda t: jnp.repeat(t, dim, axis=1)
    return cos_t, sin_t, dmask, rep(zeta), rep(xi), jnp.broadcast_to(gch[:, None, None], (H, 1, C))


def _mla_proj_kernel(cq_ref, ckv_ref, kr_ref, qg_ref, kvg_ref, wq_ref, wkv_ref, rt_ref, q_ref, k_ref, v_ref):
    hq = MLA_NOPE + 2 * MLA_ROPE
    scale = (MLA_NOPE + MLA_ROPE) ** -0.5 * math.log2(math.e)
    lane = lax.broadcasted_iota(jnp.int32, (1, 2 * MLA_ROPE), 1)
    keep = jnp.where(lane < MLA_ROPE, scale, 0.0)
    rt = rt_ref[...]

    def rope(y):
        z = y * rt
        return z + pltpu.roll(z, MLA_ROPE, 1)

    kr = rope(kr_ref[...]).astype(BF16)
    q = _dot(_rms(cq_ref[...], qg_ref[...]).astype(BF16), wq_ref[...])
    kv = _dot(_rms(ckv_ref[...], kvg_ref[...]).astype(BF16), wkv_ref[...])
    for h in range(MLA_HEADS):
        q_ref[:, h * hq:h * hq + MLA_NOPE] = (q[:, h * hq:h * hq + MLA_NOPE] * scale).astype(BF16)
        q_ref[:, h * hq + MLA_NOPE:(h + 1) * hq] = (rope(q[:, h * hq + MLA_NOPE:(h + 1) * hq]) * keep).astype(BF16)
        k_ref[:, h * hq:h * hq + MLA_NOPE] = kv[:, h * MLA_NOPE:(h + 1) * MLA_NOPE].astype(BF16)
        k_ref[:, h * hq + MLA_NOPE:(h + 1) * hq] = kr
        v0 = MLA_HEADS * MLA_NOPE + h * MLA_V
        v_ref[:, 2 * h * MLA_V:(2 * h + 1) * MLA_V] = kv[:, v0:v0 + MLA_V].astype(BF16)
        v_ref[:, (2 * h + 1) * MLA_V:(2 * h + 2) * MLA_V] = jnp.ones((kv.shape[0], MLA_V), BF16)


def _mla_proj(proj, qg, kvg, wq, wkv, rt, l, S, q_lora, kv_lora, col0):
    T = proj.shape[0]
    tm = _tile(S, 512)
    ns = S // tm
    hq = MLA_NOPE + 2 * MLA_ROPE
    c_kv0 = col0 + q_lora
    c_kr0 = c_kv0 + kv_lora
    assert col0 % q_lora == 0 and c_kv0 % kv_lora == 0 and c_kr0 % LANES == 0
    return pl.pallas_call(
        _mla_proj_kernel,
        grid=(T // tm,),
        in_specs=[
            pl.BlockSpec((tm, q_lora), lambda i: (i, col0 // q_lora)),
            pl.BlockSpec((tm, kv_lora), lambda i: (i, c_kv0 // kv_lora)),
            pl.BlockSpec((tm, 2 * MLA_ROPE), lambda i: (i, c_kr0 // LANES)),
            pl.BlockSpec((None, 1, q_lora), lambda i: (l, 0, 0)),
            pl.BlockSpec((None, 1, kv_lora), lambda i: (l, 0, 0)),
            pl.BlockSpec((None, q_lora, MLA_HEADS * hq), lambda i: (l, 0, 0)),
            pl.BlockSpec((None, kv_lora, MLA_HEADS * (MLA_NOPE + MLA_V)), lambda i: (l, 0, 0)),
            pl.BlockSpec((tm, 2 * MLA_ROPE), lambda i: (i % ns, 0)),
        ],
        out_specs=[
            pl.BlockSpec((tm, MLA_HEADS * hq), lambda i: (i, 0)),
            pl.BlockSpec((tm, MLA_HEADS * hq), lambda i: (i, 0)),
            pl.BlockSpec((tm, MLA_HEADS * 2 * MLA_V), lambda i: (i, 0)),
        ],
        out_shape=[
            jax.ShapeDtypeStruct((T, MLA_HEADS * hq), BF16),
            jax.ShapeDtypeStruct((T, MLA_HEADS * hq), BF16),
            jax.ShapeDtypeStruct((T, MLA_HEADS * 2 * MLA_V), BF16),
        ],
        compiler_params=_cparams(("parallel",)),
        name="mla_proj",
    )(proj, proj, proj, qg, kvg, wq, wkv, rt)


HEADS_PER_STEP = 2
KEY_BLOCK = 512


def _flash_kernel(q_ref, k_ref, v_ref, o_ref, *, tq):
    qi = pl.program_id(2)
    hq = MLA_NOPE + 2 * MLA_ROPE
    tk = KEY_BLOCK
    rep = tk // LANES
    heads = range(HEADS_PER_STEP)
    q = [q_ref[:, g * hq:(g + 1) * hq] for g in heads]

    def keys(j):
        return pl.ds(pl.multiple_of(j * tk, tk), tk)

    def scores(j, g):
        return _dot_nt(q[g], k_ref[keys(j), g * hq:(g + 1) * hq])

    def update(j, g, s, m, acc):
        m_new = jnp.maximum(m, jnp.max(s, axis=-1, keepdims=True))
        p = jnp.exp2(s - jnp.tile(m_new, (1, rep)))
        alpha = jnp.exp2(m - m_new)
        pv = _dot(p.astype(BF16), v_ref[keys(j), 2 * g * MLA_V:2 * (g + 1) * MLA_V])
        return m_new, jnp.tile(alpha, (1, 2)) * acc + pv

    def body(j, carry):
        return tuple(update(j, g, scores(j, g), *carry[g]) for g in heads)

    init = tuple((jnp.full((tq, LANES), -1e30, F32), jnp.zeros((tq, 2 * MLA_V), F32)) for g in heads)
    nd = tq // tk
    carry = lax.fori_loop(0, qi * nd, body, init)
    row = lax.broadcasted_iota(jnp.int32, (tq, tk), 0)
    col = lax.broadcasted_iota(jnp.int32, (tq, tk), 1)
    for g in heads:
        m, acc = carry[g]
        for d in range(nd):
            j = qi * nd + d
            m, acc = update(j, g, jnp.where(col + d * tk <= row, scores(j, g), -1e30), m, acc)
        o_ref[:, g * MLA_V:(g + 1) * MLA_V] = (acc[:, :MLA_V] / acc[:, MLA_V:]).astype(o_ref.dtype)


def _flash(q, k, v, B, S):
    T = q.shape[0]
    tq = _tile(S, 512)
    nq = S // tq
    G = HEADS_PER_STEP
    hq = MLA_NOPE + 2 * MLA_ROPE
    return pl.pallas_call(
        functools.partial(_flash_kernel, tq=tq),
        grid=(B, MLA_HEADS // G, nq),
        in_specs=[
            pl.BlockSpec((tq, G * hq), lambda b, h, i: (b * nq + i, h)),
            pl.BlockSpec((S, G * hq), lambda b, h, i: (b, h)),
            pl.BlockSpec((S, G * 2 * MLA_V), lambda b, h, i: (b, h)),
        ],
        out_specs=pl.BlockSpec((tq, G * MLA_V), lambda b, h, i: (b * nq + i, h)),
        out_shape=jax.ShapeDtypeStruct((T, MLA_HEADS * MLA_V), BF16),
        compiler_params=_cparams(("parallel", "parallel", "arbitrary")),
        name="mla_flash",
    )(q, k, v)


def _mla_rope_table(S):
    half = MLA_ROPE // 2
    pos = jnp.arange(S, dtype=F32)
    inv_freq = ROPE_BASE ** (-jnp.arange(half, dtype=F32) * 2.0 / MLA_ROPE)
    ang = pos[:, None] * inv_freq[None, :]
    cos, sin = jnp.cos(ang), jnp.sin(ang)
    return jnp.concatenate([cos, cos, -sin, sin], axis=-1)


def _swap_halves(w):
    half = w.shape[-1] // 2
    return jnp.concatenate([w[..., half:], w[..., :half]], axis=-1)


def _xattn_kernel(q_ref, k_ref, v_ref, o_ref, *, hd):
    scale = hd ** -0.5
    for h in range(XA_HEADS):
        cols = slice(h * hd, (h + 1) * hd)
        s = _dot_nt(q_ref[:, cols], k_ref[:, cols]) * scale
        e = jnp.exp(s - jnp.max(s, axis=-1, keepdims=True))
        p = e / jnp.sum(e, axis=-1, keepdims=True)
        o_ref[:, cols] = _dot(p.astype(BF16), v_ref[:, cols]).astype(o_ref.dtype)


def _xattn(q, kv, S, M):
    T, D = q.shape
    tm = _tile(S, 512)
    ns = S // tm
    return pl.pallas_call(
        functools.partial(_xattn_kernel, hd=D // XA_HEADS),
        grid=(T // tm,),
        in_specs=[
            pl.BlockSpec((tm, D), lambda i: (i, 0)),
            pl.BlockSpec((M, D), lambda i: (i // ns, 0)),
            pl.BlockSpec((M, D), lambda i: (i // ns, 1)),
        ],
        out_specs=pl.BlockSpec((tm, D), lambda i: (i, 0)),
        out_shape=jax.ShapeDtypeStruct((T, D), BF16),
        compiler_params=_cparams(("parallel",)),
        name="xattn",
    )(q, kv, kv)


def kernel(x, mem, ffn1_norm, ffn1_w_gate, ffn1_w_up, ffn1_w_down, mix_norm, w_in, sgu_norm, sgu_w_s, sgu_b, ret_gn, q_norm, w_uq, kv_norm, w_ukv, w_out, xa_norm, mem_norm, xa_wq, xa_wkv, xa_wo, ffn2_norm, ffn2_w_gate, ffn2_w_up, ffn2_w_down, final_norm):
    B, S, D = x.shape
    M = mem.shape[1]
    L = w_in.shape[0]
    T = B * S
    a_width = sgu_norm.shape[-1]
    r_width = ret_gn.shape[-1]
    q_lora = q_norm.shape[-1]
    kv_lora = kv_norm.shape[-1]
    assert a_width == A_GROUPS * CHUNK and r_width == R_HEADS * CHUNK and S % CHUNK == 0
    ret_col0 = 2 * a_width
    mla_col0 = ret_col0 + 4 * r_width
    kr_col0 = mla_col0 + q_lora + kv_lora
    assert w_in.shape[-1] == kr_col0 + MLA_ROPE

    bf = lambda w: w.astype(BF16)
    row = lambda g: g.reshape(g.shape[0], 1, g.shape[1])

    n_in = kr_col0 + 2 * MLA_ROPE
    n_in_pad = -(-n_in // 1024) * 1024
    w_in_x = jnp.concatenate(
        [bf(w_in), bf(_swap_halves(w_in[..., kr_col0:])), jnp.zeros((L, D, n_in_pad - n_in), BF16)], axis=-1)
    wq4 = w_uq.reshape(L, q_lora, MLA_HEADS, MLA_NOPE + MLA_ROPE)
    wq_x = bf(jnp.concatenate([wq4, _swap_halves(wq4[..., MLA_NOPE:])], axis=-1)).reshape(L, q_lora, -1)
    wkv4 = w_ukv.reshape(L, kv_lora, MLA_HEADS, MLA_NOPE + MLA_V)
    wkv_x = bf(jnp.concatenate([wkv4[..., :MLA_NOPE].reshape(L, kv_lora, -1),
                                wkv4[..., MLA_NOPE:].reshape(L, kv_lora, -1)], axis=-1))
    sgu_bt = jnp.swapaxes(sgu_b, 1, 2)
    ret_tabs = _retention_tables(S)
    mla_rt = _mla_rope_table(S)
    w1g, w1u, w1d = bf(ffn1_w_gate), bf(ffn1_w_up), bf(ffn1_w_down)
    w2g, w2u, w2d = bf(ffn2_w_gate), bf(ffn2_w_up), bf(ffn2_w_down)
    w_out_b, xa_wq_b, xa_wkv_b, xa_wo_b = bf(w_out), bf(xa_wq), bf(xa_wkv), bf(xa_wo)
    fin = final_norm.reshape(1, D)

    h = x.reshape(T, D)
    mem2 = mem.reshape(B * M, D)
    for l in range(L):
        h = _ffn(h, row(ffn1_norm), w1g, w1u, w1d, l, fin, False)
        proj = _norm_mm(h, row(mix_norm), w_in_x, l, F32)
        y_a = _sgu(proj, row(sgu_norm), sgu_w_s, sgu_bt, l, a_width)
        y_r = _retention(proj, *ret_tabs, row(ret_gn), l, B, S, r_width, ret_col0)
        q, k, v = _mla_proj(proj, row(q_norm), row(kv_norm), wq_x, wkv_x, mla_rt, l, S, q_lora, kv_lora, mla_col0)
        y_c = _flash(q, k, v, B, S)
        h = _mm_res([y_a, y_r, y_c], w_out_b, l, h)
        kvm = _norm_mm(mem2, row(mem_norm), xa_wkv_b, l, BF16)
        xq = _norm_mm(h, row(xa_norm), xa_wq_b, l, BF16)
        o = _xattn(xq, kvm, S, M)
        h = _mm_res([o], xa_wo_b, l, h)
        h = _ffn(h, row(ffn2_norm), w2g, w2u, w2d, l, fin, l == L - 1)
    return h.reshape(B, S, D)
```

```python
import functools
import math

import jax
import jax.numpy as jnp
from jax import lax
from jax.experimental import pallas as pl
from jax.experimental.pallas import tpu as pltpu

F32 = jnp.float32
BF16 = jnp.bfloat16

EPS = 1e-6
ROPE_BASE = 10000.0
CHUNK = 128
A_GROUPS = 4
R_HEADS = 4
MLA_HEADS = 8
MLA_NOPE = 128
MLA_ROPE = 64
MLA_V = 128
XA_HEADS = 4
LANES = 128

VMEM_LIMIT = 56 * 1024 * 1024


def _cparams(sem):
    return pltpu.CompilerParams(dimension_semantics=sem, vmem_limit_bytes=VMEM_LIMIT)


def _tile(n, pref):
    t = min(n, pref)
    assert n % t == 0, (n, t)
    return t


def _rms(x, g):
    return x * lax.rsqrt(jnp.mean(x * x, axis=-1, keepdims=True) + EPS) * g


def _silu(x):
    return x * (1.0 / (1.0 + jnp.exp(-x)))


def _dot(a, b):
    return jnp.dot(a, b, preferred_element_type=F32)


def _dot_nt(a, b):
    return lax.dot_general(a, b, (((1,), (1,)), ((), ())), preferred_element_type=F32)


def _ffn_kernel(h_ref, g_ref, wg_ref, wu_ref, wd_ref, fg_ref, o_ref, n_ref, acc_ref, *, nf, final):
    f = pl.program_id(1)

    @pl.when(f == 0)
    def _():
        n_ref[...] = _rms(h_ref[...], g_ref[...]).astype(BF16)
        acc_ref[...] = jnp.zeros_like(acc_ref)

    n = n_ref[...]
    a = _dot(n, wg_ref[...])
    b = _dot(n, wu_ref[...])
    acc_ref[...] += _dot((_silu(a) * b).astype(BF16), wd_ref[...])

    @pl.when(f == nf - 1)
    def _():
        y = h_ref[...] + 0.5 * acc_ref[...]
        if final:
            y = _rms(y, fg_ref[...])
        o_ref[...] = y


def _ffn(h, g, wg, wu, wd, l, final_g, final):
    T, D = h.shape
    F = wg.shape[-1]
    tm = _tile(T, 512)
    tf = _tile(F, 512)
    nf = F // tf
    return pl.pallas_call(
        functools.partial(_ffn_kernel, nf=nf, final=final),
        grid=(T // tm, nf),
        in_specs=[
            pl.BlockSpec((tm, D), lambda i, f: (i, 0)),
            pl.BlockSpec((None, 1, D), lambda i, f: (l, 0, 0)),
            pl.BlockSpec((None, D, tf), lambda i, f: (l, 0, f)),
            pl.BlockSpec((None, D, tf), lambda i, f: (l, 0, f)),
            pl.BlockSpec((None, tf, D), lambda i, f: (l, f, 0)),
            pl.BlockSpec((1, D), lambda i, f: (0, 0)),
        ],
        out_specs=pl.BlockSpec((tm, D), lambda i, f: (i, 0)),
        out_shape=jax.ShapeDtypeStruct((T, D), F32),
        scratch_shapes=[pltpu.VMEM((tm, D), BF16), pltpu.VMEM((tm, D), F32)],
        compiler_params=_cparams(("parallel", "arbitrary")),
        name="ffn",
    )(h, g, wg, wu, wd, final_g)


COL_CHUNK = 1024


def _resident(block_shape, index_map):
    return pl.BlockSpec(block_shape, index_map, pipeline_mode=pl.Buffered(1))


def _norm_mm_kernel(x_ref, g_ref, w_ref, o_ref):
    n = _rms(x_ref[...], g_ref[...]).astype(BF16)
    N = o_ref.shape[1]
    tn = min(N, COL_CHUNK)
    for c in range(N // tn):
        o_ref[:, c * tn:(c + 1) * tn] = _dot(n, w_ref[:, c * tn:(c + 1) * tn]).astype(o_ref.dtype)


def _norm_mm(x, g, w, l, out_dtype, tm_pref=512):
    M, K = x.shape
    N = w.shape[-1]
    tm = _tile(M, tm_pref)
    return pl.pallas_call(
        _norm_mm_kernel,
        grid=(M // tm,),
        in_specs=[
            pl.BlockSpec((tm, K), lambda i: (i, 0)),
            _resident((None, 1, K), lambda i: (l, 0, 0)),
            _resident((None, K, N), lambda i: (l, 0, 0)),
        ],
        out_specs=pl.BlockSpec((tm, N), lambda i: (i, 0)),
        out_shape=jax.ShapeDtypeStruct((M, N), out_dtype),
        compiler_params=_cparams(("parallel",)),
        name="norm_mm",
    )(x, g, w)


def _mm_res_kernel(*refs, ny):
    ys, ws, h_ref, o_ref = refs[:ny], refs[ny:2 * ny], refs[2 * ny], refs[2 * ny + 1]
    N = o_ref.shape[1]
    tn = min(N, COL_CHUNK)
    for c in range(N // tn):
        cols = slice(c * tn, (c + 1) * tn)
        acc = h_ref[:, cols]
        for y_ref, w_ref in zip(ys, ws):
            acc = acc + _dot(y_ref[...], w_ref[:, cols])
        o_ref[:, cols] = acc


def _mm_res(ys, w, l, h, tm_pref=512):
    T, N = h.shape
    tm = _tile(T, tm_pref)
    y_specs, w_specs, row = [], [], 0
    for y in ys:
        k = y.shape[1]
        assert row % k == 0
        y_specs.append(pl.BlockSpec((tm, k), lambda i: (i, 0)))
        w_specs.append(_resident((None, k, N), functools.partial(lambda i, rb: (l, rb, 0), rb=row // k)))
        row += k
    assert row == w.shape[1]
    return pl.pallas_call(
        functools.partial(_mm_res_kernel, ny=len(ys)),
        grid=(T // tm,),
        in_specs=y_specs + w_specs + [pl.BlockSpec((tm, N), lambda i: (i, 0))],
        out_specs=pl.BlockSpec((tm, N), lambda i: (i, 0)),
        out_shape=jax.ShapeDtypeStruct((T, N), F32),
        compiler_params=_cparams(("parallel",)),
        name="mm_res",
    )(*ys, *([w] * len(ys)), h)


def _sgu_kernel(u_ref, v_ref, g_ref, ws_ref, bt_ref, o_ref, *, nchunk):
    C = CHUNK
    row = lax.broadcasted_iota(jnp.int32, (C, C), 0)
    col = lax.broadcasted_iota(jnp.int32, (C, C), 1)
    wm = [jnp.where(row >= col, ws_ref[g], 0.0).astype(BF16) for g in range(A_GROUPS)]
    bt = bt_ref[...]
    for c in range(nchunk):
        rows = slice(c * C, (c + 1) * C)
        u = jax.nn.gelu(u_ref[rows, :])
        v = _rms(jax.nn.gelu(v_ref[rows, :]), g_ref[...]).astype(BF16)
        for g in range(A_GROUPS):
            cols = slice(g * C, (g + 1) * C)
            z = _dot(wm[g], v[:, cols]) + bt[:, g:g + 1]
            o_ref[rows, cols] = (u[:, cols] * z).astype(o_ref.dtype)


def _sgu(proj, g, ws, bt, l, width):
    T = proj.shape[0]
    ts = _tile(T, 4 * CHUNK)
    return pl.pallas_call(
        functools.partial(_sgu_kernel, nchunk=ts // CHUNK),
        grid=(T // ts,),
        in_specs=[
            pl.BlockSpec((ts, width), lambda i: (i, 0)),
            pl.BlockSpec((ts, width), lambda i: (i, 1)),
            pl.BlockSpec((None, 1, width), lambda i: (l, 0, 0)),
            pl.BlockSpec((None, A_GROUPS, CHUNK, CHUNK), lambda i: (l, 0, 0, 0)),
            pl.BlockSpec((None, CHUNK, A_GROUPS), lambda i: (l, 0, 0)),
        ],
        out_specs=pl.BlockSpec((ts, width), lambda i: (i, 0)),
        out_shape=jax.ShapeDtypeStruct((T, width), BF16),
        compiler_params=_cparams(("parallel",)),
        name="sgu",
    )(proj, proj, g, ws, bt)


def _ret_kernel(q_ref, k_ref, v_ref, gate_ref, cos_ref, sin_ref, dmask_ref, zeta_ref, xi_ref, gch_ref, gn_ref,
                o_ref, state_ref, *, nchunk):
    C = CHUNK
    hd = C
    half = hd // 2

    @pl.when(pl.program_id(1) == 0)
    def _():
        state_ref[...] = jnp.zeros_like(state_ref)

    for c in range(nchunk):
        rows = slice(c * C, (c + 1) * C)
        cos = cos_ref[rows, :]
        sin = sin_ref[rows, :]
        for h in range(R_HEADS):
            cols = slice(h * hd, (h + 1) * hd)
            q = q_ref[rows, cols]
            k = k_ref[rows, cols]
            q = q * cos + pltpu.roll(q, half, 1) * sin
            k = (k * cos + pltpu.roll(k, half, 1) * sin) * (hd ** -0.5)
            vb = v_ref[rows, cols].astype(BF16)
            scores = _dot_nt(q.astype(BF16), k.astype(BF16)) * dmask_ref[h]
            state = state_ref[h]
            y = _dot(scores.astype(BF16), vb) + _dot((q * xi_ref[:, cols]).astype(BF16), state.astype(BF16))
            kz = (k * zeta_ref[:, cols]).astype(BF16)
            state_ref[h] = gch_ref[h] * state + _dot(kz.T, vb)
            mu = jnp.mean(y, axis=-1, keepdims=True)
            yc = y - mu
            var = jnp.mean(yc * yc, axis=-1, keepdims=True)
            yn = yc * lax.rsqrt(var + EPS) * gn_ref[:, cols]
            o_ref[rows, cols] = (_silu(gate_ref[rows, cols]) * yn).astype(o_ref.dtype)


def _retention(proj, cos, sin, dmask, zeta, xi, gch, gn, l, B, S, width, col0):
    T = proj.shape[0]
    tr = _tile(S, 4 * CHUNK)
    ns = S // tr
    cb = col0 // width
    assert col0 % width == 0
    tok = lambda b, n: b * ns + n
    return pl.pallas_call(
        functools.partial(_ret_kernel, nchunk=tr // CHUNK),
        grid=(B, ns),
        in_specs=[
            pl.BlockSpec((tr, width), lambda b, n: (tok(b, n), cb)),
            pl.BlockSpec((tr, width), lambda b, n: (tok(b, n), cb + 1)),
            pl.BlockSpec((tr, width), lambda b, n: (tok(b, n), cb + 2)),
            pl.BlockSpec((tr, width), lambda b, n: (tok(b, n), cb + 3)),
            pl.BlockSpec((tr, CHUNK), lambda b, n: (n, 0)),
            pl.BlockSpec((tr, CHUNK), lambda b, n: (n, 0)),
            pl.BlockSpec((R_HEADS, CHUNK, CHUNK), lambda b, n: (0, 0, 0)),
            pl.BlockSpec((CHUNK, width), lambda b, n: (0, 0)),
            pl.BlockSpec((CHUNK, width), lambda b, n: (0, 0)),
            pl.BlockSpec((R_HEADS, 1, CHUNK), lambda b, n: (0, 0, 0)),
            pl.BlockSpec((None, 1, width), lambda b, n: (l, 0, 0)),
        ],
        out_specs=pl.BlockSpec((tr, width), lambda b, n: (tok(b, n), 0)),
        out_shape=jax.ShapeDtypeStruct((T, width), BF16),
        scratch_shapes=[pltpu.VMEM((R_HEADS, CHUNK, CHUNK), F32)],
        compiler_params=_cparams(("parallel", "arbitrary")),
        name="retention",
    )(proj, proj, proj, proj, cos, sin, dmask, zeta, xi, gch, gn)


def _retention_tables(S):
    C, H, dim = CHUNK, R_HEADS, CHUNK
    half = dim // 2
    pos = jnp.arange(S, dtype=F32)
    inv_freq = ROPE_BASE ** (-jnp.arange(half, dtype=F32) * 2.0 / dim)
    ang = pos[:, None] * inv_freq[None, :]
    cos, sin = jnp.cos(ang), jnp.sin(ang)
    cos_t = jnp.concatenate([cos, cos], axis=-1)
    sin_t = jnp.concatenate([-sin, sin], axis=-1)
    log_g = jnp.log1p(-jnp.exp2(-5.0 - jnp.arange(H, dtype=F32)))
    i = jnp.arange(C, dtype=F32)
    diff = i[:, None] - i[None, :]
    dmask = jnp.where(diff[None] >= 0, jnp.exp(jnp.maximum(diff, 0.0)[None] * log_g[:, None, None]), 0.0)
    zeta = jnp.exp((C - 1 - i)[None, :] * log_g[:, None]).T
    xi = jnp.exp((i + 1)[None, :] * log_g[:, None]).T
    gch = jnp.exp(C * log_g)
    rep = lambda t: jnp.repeat(t, dim, axis=1)
    return cos_t, sin_t, dmask, rep(zeta), rep(xi), jnp.broadcast_to(gch[:, None, None], (H, 1, C))


def _mla_proj_kernel(cq_ref, ckv_ref, kr_ref, qg_ref, kvg_ref, wq_ref, wkv_ref, rt_ref, q_ref, k_ref, v_ref):
    hq = MLA_NOPE + 2 * MLA_ROPE
    scale = (MLA_NOPE + MLA_ROPE) ** -0.5 * math.log2(math.e)
    lane = lax.broadcasted_iota(jnp.int32, (1, 2 * MLA_ROPE), 1)
    keep = jnp.where(lane < MLA_ROPE, scale, 0.0)
    rt = rt_ref[...]

    def rope(y):
        z = y * rt
        return z + pltpu.roll(z, MLA_ROPE, 1)

    kr = rope(kr_ref[...]).astype(BF16)
    q = _dot(_rms(cq_ref[...], qg_ref[...]).astype(BF16), wq_ref[...])
    kv = _dot(_rms(ckv_ref[...], kvg_ref[...]).astype(BF16), wkv_ref[...])
    for h in range(MLA_HEADS):
        q_ref[:, h * hq:h * hq + MLA_NOPE] = (q[:, h * hq:h * hq + MLA_NOPE] * scale).astype(BF16)
        q_ref[:, h * hq + MLA_NOPE:(h + 1) * hq] = (rope(q[:, h * hq + MLA_NOPE:(h + 1) * hq]) * keep).astype(BF16)
        k_ref[:, h * hq:h * hq + MLA_NOPE] = kv[:, h * MLA_NOPE:(h + 1) * MLA_NOPE].astype(BF16)
        k_ref[:, h * hq + MLA_NOPE:(h + 1) * hq] = kr
        v0 = MLA_HEADS * MLA_NOPE + h * MLA_V
        v_ref[:, 2 * h * MLA_V:(2 * h + 1) * MLA_V] = kv[:, v0:v0 + MLA_V].astype(BF16)
        v_ref[:, (2 * h + 1) * MLA_V:(2 * h + 2) * MLA_V] = jnp.ones((kv.shape[0], MLA_V), BF16)


def _mla_proj(proj, qg, kvg, wq, wkv, rt, l, S, q_lora, kv_lora, col0):
    T = proj.shape[0]
    tm = _tile(S, 512)
    ns = S // tm
    hq = MLA_NOPE + 2 * MLA_ROPE
    c_kv0 = col0 + q_lora
    c_kr0 = c_kv0 + kv_lora
    assert col0 % q_lora == 0 and c_kv0 % kv_lora == 0 and c_kr0 % LANES == 0
    return pl.pallas_call(
        _mla_proj_kernel,
        grid=(T // tm,),
        in_specs=[
            pl.BlockSpec((tm, q_lora), lambda i: (i, col0 // q_lora)),
            pl.BlockSpec((tm, kv_lora), lambda i: (i, c_kv0 // kv_lora)),
            pl.BlockSpec((tm, 2 * MLA_ROPE), lambda i: (i, c_kr0 // LANES)),
            pl.BlockSpec((None, 1, q_lora), lambda i: (l, 0, 0)),
            pl.BlockSpec((None, 1, kv_lora), lambda i: (l, 0, 0)),
            pl.BlockSpec((None, q_lora, MLA_HEADS * hq), lambda i: (l, 0, 0)),
            pl.BlockSpec((None, kv_lora, MLA_HEADS * (MLA_NOPE + MLA_V)), lambda i: (l, 0, 0)),
            pl.BlockSpec((tm, 2 * MLA_ROPE), lambda i: (i % ns, 0)),
        ],
        out_specs=[
            pl.BlockSpec((tm, MLA_HEADS * hq), lambda i: (i, 0)),
            pl.BlockSpec((tm, MLA_HEADS * hq), lambda i: (i, 0)),
            pl.BlockSpec((tm, MLA_HEADS * 2 * MLA_V), lambda i: (i, 0)),
        ],
        out_shape=[
            jax.ShapeDtypeStruct((T, MLA_HEADS * hq), BF16),
            jax.ShapeDtypeStruct((T, MLA_HEADS * hq), BF16),
            jax.ShapeDtypeStruct((T, MLA_HEADS * 2 * MLA_V), BF16),
        ],
        compiler_params=_cparams(("parallel",)),
        name="mla_proj",
    )(proj, proj, proj, qg, kvg, wq, wkv, rt)


def _flash_kernel(q_ref, k_ref, v_ref, o_ref, s_ref, mc_ref, m_ref, acc_ref, *, tq):
    qi = pl.program_id(2)
    rep = tq // LANES
    q = q_ref[...]

    def keys(j):
        return pl.ds(pl.multiple_of(j * tq, tq), tq)

    def rowmax(s):
        return jnp.broadcast_to(jnp.max(s, axis=-1, keepdims=True), (tq, LANES))

    def scores(j, slot):
        s = _dot_nt(q, k_ref[keys(j), :])
        s_ref[slot] = s
        mc_ref[slot] = rowmax(s)

    def update(j, slot, diagonal=False):
        s = s_ref[slot]
        if diagonal:
            row = lax.broadcasted_iota(jnp.int32, s.shape, 0)
            col = lax.broadcasted_iota(jnp.int32, s.shape, 1)
            s = jnp.where(col <= row, s, -1e30)
            m_cur = rowmax(s)
        else:
            m_cur = mc_ref[slot]
        m = m_ref[...]
        m_new = jnp.maximum(m, m_cur)
        m_ref[...] = m_new
        p = jnp.exp2(s - jnp.tile(m_new, (1, rep)))
        alpha = jnp.exp2(m - m_new)
        acc_ref[...] = jnp.tile(alpha, (1, 2)) * acc_ref[...] + _dot(p.astype(BF16), v_ref[keys(j), :])

    def finish():
        o_ref[...] = (acc_ref[:, :MLA_V] / acc_ref[:, MLA_V:]).astype(o_ref.dtype)

    m_ref[...] = jnp.full(m_ref.shape, -1e30, F32)
    acc_ref[...] = jnp.zeros(acc_ref.shape, F32)
    scores(0, 0)

    @pl.loop(0, qi // 2)
    def _(t):
        scores(2 * t + 1, 1)
        update(2 * t, 0)
        scores(2 * t + 2, 0)
        update(2 * t + 1, 1)

    @pl.when(qi % 2 == 1)
    def _():
        scores(qi, 1)
        update(qi - 1, 0)
        update(qi, 1, diagonal=True)
        finish()

    @pl.when(qi % 2 == 0)
    def _():
        update(qi, 0, diagonal=True)
        finish()


def _flash(q, k, v, B, S):
    T = q.shape[0]
    tq = _tile(S, 512)
    nq = S // tq
    hq = MLA_NOPE + 2 * MLA_ROPE
    return pl.pallas_call(
        functools.partial(_flash_kernel, tq=tq),
        grid=(B, MLA_HEADS, nq),
        in_specs=[
            pl.BlockSpec((tq, hq), lambda b, h, i: (b * nq + i, h)),
            pl.BlockSpec((S, hq), lambda b, h, i: (b, h)),
            pl.BlockSpec((S, 2 * MLA_V), lambda b, h, i: (b, h)),
        ],
        out_specs=pl.BlockSpec((tq, MLA_V), lambda b, h, i: (b * nq + i, h)),
        out_shape=jax.ShapeDtypeStruct((T, MLA_HEADS * MLA_V), BF16),
        scratch_shapes=[
            pltpu.VMEM((2, tq, tq), F32),
            pltpu.VMEM((2, tq, LANES), F32),
            pltpu.VMEM((tq, LANES), F32),
            pltpu.VMEM((tq, 2 * MLA_V), F32),
        ],
        compiler_params=_cparams(("parallel", "parallel", "arbitrary")),
        name="mla_flash",
    )(q, k, v)


def _mla_rope_table(S):
    half = MLA_ROPE // 2
    pos = jnp.arange(S, dtype=F32)
    inv_freq = ROPE_BASE ** (-jnp.arange(half, dtype=F32) * 2.0 / MLA_ROPE)
    ang = pos[:, None] * inv_freq[None, :]
    cos, sin = jnp.cos(ang), jnp.sin(ang)
    return jnp.concatenate([cos, cos, -sin, sin], axis=-1)


def _swap_halves(w):
    half = w.shape[-1] // 2
    return jnp.concatenate([w[..., half:], w[..., :half]], axis=-1)


def _xattn_kernel(q_ref, k_ref, v_ref, o_ref, *, hd):
    scale = hd ** -0.5
    for h in range(XA_HEADS):
        cols = slice(h * hd, (h + 1) * hd)
        s = _dot_nt(q_ref[:, cols], k_ref[:, cols]) * scale
        e = jnp.exp(s - jnp.max(s, axis=-1, keepdims=True))
        p = e / jnp.sum(e, axis=-1, keepdims=True)
        o_ref[:, cols] = _dot(p.astype(BF16), v_ref[:, cols]).astype(o_ref.dtype)


def _xattn(q, kv, S, M):
    T, D = q.shape
    tm = _tile(S, 512)
    ns = S // tm
    return pl.pallas_call(
        functools.partial(_xattn_kernel, hd=D // XA_HEADS),
        grid=(T // tm,),
        in_specs=[
            pl.BlockSpec((tm, D), lambda i: (i, 0)),
            pl.BlockSpec((M, D), lambda i: (i // ns, 0)),
            pl.BlockSpec((M, D), lambda i: (i // ns, 1)),
        ],
        out_specs=pl.BlockSpec((tm, D), lambda i: (i, 0)),
        out_shape=jax.ShapeDtypeStruct((T, D), BF16),
        compiler_params=_cparams(("parallel",)),
        name="xattn",
    )(q, kv, kv)


def kernel(x, mem, ffn1_norm, ffn1_w_gate, ffn1_w_up, ffn1_w_down, mix_norm, w_in, sgu_norm, sgu_w_s, sgu_b, ret_gn, q_norm, w_uq, kv_norm, w_ukv, w_out, xa_norm, mem_norm, xa_wq, xa_wkv, xa_wo, ffn2_norm, ffn2_w_gate, ffn2_w_up, ffn2_w_down, final_norm):
    B, S, D = x.shape
    M = mem.shape[1]
    L = w_in.shape[0]
    T = B * S
    a_width = sgu_norm.shape[-1]
    r_width = ret_gn.shape[-1]
    q_lora = q_norm.shape[-1]
    kv_lora = kv_norm.shape[-1]
    assert a_width == A_GROUPS * CHUNK and r_width == R_HEADS * CHUNK and S % CHUNK == 0
    ret_col0 = 2 * a_width
    mla_col0 = ret_col0 + 4 * r_width
    kr_col0 = mla_col0 + q_lora + kv_lora
    assert w_in.shape[-1] == kr_col0 + MLA_ROPE

    bf = lambda w: w.astype(BF16)
    row = lambda g: g.reshape(g.shape[0], 1, g.shape[1])

    n_in = kr_col0 + 2 * MLA_ROPE
    n_in_pad = -(-n_in // 1024) * 1024
    w_in_x = jnp.concatenate(
        [bf(w_in), bf(_swap_halves(w_in[..., kr_col0:])), jnp.zeros((L, D, n_in_pad - n_in), BF16)], axis=-1)
    wq4 = w_uq.reshape(L, q_lora, MLA_HEADS, MLA_NOPE + MLA_ROPE)
    wq_x = bf(jnp.concatenate([wq4, _swap_halves(wq4[..., MLA_NOPE:])], axis=-1)).reshape(L, q_lora, -1)
    wkv4 = w_ukv.reshape(L, kv_lora, MLA_HEADS, MLA_NOPE + MLA_V)
    wkv_x = bf(jnp.concatenate([wkv4[..., :MLA_NOPE].reshape(L, kv_lora, -1),
                                wkv4[..., MLA_NOPE:].reshape(L, kv_lora, -1)], axis=-1))
    sgu_bt = jnp.swapaxes(sgu_b, 1, 2)
    ret_tabs = _retention_tables(S)
    mla_rt = _mla_rope_table(S)
    w1g, w1u, w1d = bf(ffn1_w_gate), bf(ffn1_w_up), bf(ffn1_w_down)
    w2g, w2u, w2d = bf(ffn2_w_gate), bf(ffn2_w_up), bf(ffn2_w_down)
    w_out_b, xa_wq_b, xa_wkv_b, xa_wo_b = bf(w_out), bf(xa_wq), bf(xa_wkv), bf(xa_wo)
    fin = final_norm.reshape(1, D)

    h = x.reshape(T, D)
    mem2 = mem.reshape(B * M, D)
    for l in range(L):
        h = _ffn(h, row(ffn1_norm), w1g, w1u, w1d, l, fin, False)
        proj = _norm_mm(h, row(mix_norm), w_in_x, l, F32)
        y_a = _sgu(proj, row(sgu_norm), sgu_w_s, sgu_bt, l, a_width)
        y_r = _retention(proj, *ret_tabs, row(ret_gn), l, B, S, r_width, ret_col0)
        q, k, v = _mla_proj(proj, row(q_norm), row(kv_norm), wq_x, wkv_x, mla_rt, l, S, q_lora, kv_lora, mla_col0)
        y_c = _flash(q, k, v, B, S)
        h = _mm_res([y_a, y_r, y_c], w_out_b, l, h)
        kvm = _norm_mm(mem2, row(mem_norm), xa_wkv_b, l, BF16)
        xq = _norm_mm(h, row(xa_norm), xa_wq_b, l, BF16)
        o = _xattn(xq, kvm, S, M)
        h = _mm_res([o], xa_wo_b, l, h)
        h = _ffn(h, row(ffn2_norm), w2g, w2u, w2d, l, fin, l == L - 1)
    return h.reshape(B, S, D)
```

```python
import functools
import math

import jax
import jax.numpy as jnp
from jax import lax
from jax.experimental import pallas as pl
from jax.experimental.pallas import tpu as pltpu

F32 = jnp.float32
BF16 = jnp.bfloat16

EPS = 1e-6
ROPE_BASE = 10000.0
CHUNK = 128
A_GROUPS = 4
R_HEADS = 4
MLA_HEADS = 8
MLA_NOPE = 128
MLA_ROPE = 64
MLA_V = 128
XA_HEADS = 4
LANES = 128

VMEM_LIMIT = 56 * 1024 * 1024
FFN_VMEM_LIMIT = 62 * 1024 * 1024


def _cparams(sem, vmem_limit=VMEM_LIMIT):
    return pltpu.CompilerParams(dimension_semantics=sem, vmem_limit_bytes=vmem_limit)


def _tile(n, pref):
    t = min(n, pref)
    assert n % t == 0, (n, t)
    return t


def _rms(x, g):
    return x * lax.rsqrt(jnp.mean(x * x, axis=-1, keepdims=True) + EPS) * g


def _silu(x):
    return x * (1.0 / (1.0 + jnp.exp(-x)))


def _dot(a, b):
    return jnp.dot(a, b, preferred_element_type=F32)


def _dot_nt(a, b):
    return lax.dot_general(a, b, (((1,), (1,)), ((), ())), preferred_element_type=F32)


COL_CHUNK = 1024
ROW_CHUNK = 256


def _ffn_kernel(h_ref, g_ref, wg_ref, wu_ref, wd_ref, fg_ref, o_ref, n_ref, *, nf, final):
    f = pl.program_id(1)
    tm, D = o_ref.shape
    row_chunks = [slice(r, r + ROW_CHUNK) for r in range(0, tm, ROW_CHUNK)]

    @pl.when(f == 0)
    def _():
        for rows in row_chunks:
            n_ref[rows, :] = _rms(h_ref[rows, :], g_ref[...]).astype(BF16)
        o_ref[...] = jnp.zeros_like(o_ref)

    n = n_ref[...]
    a = _dot(n, wg_ref[...])
    b = _dot(n, wu_ref[...])
    hm = (_silu(a) * b).astype(BF16)
    for c in range(0, D, COL_CHUNK):
        o_ref[:, c:c + COL_CHUNK] += _dot(hm, wd_ref[:, c:c + COL_CHUNK])

    @pl.when(f == nf - 1)
    def _():
        for rows in row_chunks:
            y = h_ref[rows, :] + 0.5 * o_ref[rows, :]
            if final:
                y = _rms(y, fg_ref[...])
            o_ref[rows, :] = y


def _ffn(h, g, wg, wu, wd, l, final_g, final):
    T, D = h.shape
    F = wg.shape[-1]
    tm = _tile(T, 1024)
    tf = _tile(F, 512)
    nf = F // tf
    return pl.pallas_call(
        functools.partial(_ffn_kernel, nf=nf, final=final),
        grid=(T // tm, nf),
        in_specs=[
            pl.BlockSpec((tm, D), lambda i, f: (i, 0)),
            pl.BlockSpec((None, 1, D), lambda i, f: (l, 0, 0)),
            pl.BlockSpec((None, D, tf), lambda i, f: (l, 0, f)),
            pl.BlockSpec((None, D, tf), lambda i, f: (l, 0, f)),
            pl.BlockSpec((None, tf, D), lambda i, f: (l, f, 0)),
            pl.BlockSpec((1, D), lambda i, f: (0, 0)),
        ],
        out_specs=pl.BlockSpec((tm, D), lambda i, f: (i, 0)),
        out_shape=jax.ShapeDtypeStruct((T, D), F32),
        scratch_shapes=[pltpu.VMEM((tm, D), BF16)],
        compiler_params=_cparams(("parallel", "arbitrary"), FFN_VMEM_LIMIT),
        name="ffn",
    )(h, g, wg, wu, wd, final_g)


def _resident(block_shape, index_map):
    return pl.BlockSpec(block_shape, index_map, pipeline_mode=pl.Buffered(1))


def _norm_mm_kernel(x_ref, g_ref, w_ref, o_ref):
    n = _rms(x_ref[...], g_ref[...]).astype(BF16)
    N = o_ref.shape[1]
    tn = min(N, COL_CHUNK)
    for c in range(N // tn):
        o_ref[:, c * tn:(c + 1) * tn] = _dot(n, w_ref[:, c * tn:(c + 1) * tn]).astype(o_ref.dtype)


def _norm_mm(x, g, w, l, out_dtype, tm_pref=512):
    M, K = x.shape
    N = w.shape[-1]
    tm = _tile(M, tm_pref)
    return pl.pallas_call(
        _norm_mm_kernel,
        grid=(M // tm,),
        in_specs=[
            pl.BlockSpec((tm, K), lambda i: (i, 0)),
            _resident((None, 1, K), lambda i: (l, 0, 0)),
            _resident((None, K, N), lambda i: (l, 0, 0)),
        ],
        out_specs=pl.BlockSpec((tm, N), lambda i: (i, 0)),
        out_shape=jax.ShapeDtypeStruct((M, N), out_dtype),
        compiler_params=_cparams(("parallel",)),
        name="norm_mm",
    )(x, g, w)


def _mm_res_kernel(*refs, ny):
    ys, ws, h_ref, o_ref = refs[:ny], refs[ny:2 * ny], refs[2 * ny], refs[2 * ny + 1]
    N = o_ref.shape[1]
    tn = min(N, COL_CHUNK)
    for c in range(N // tn):
        cols = slice(c * tn, (c + 1) * tn)
        acc = h_ref[:, cols]
        for y_ref, w_ref in zip(ys, ws):
            acc = acc + _dot(y_ref[...], w_ref[:, cols])
        o_ref[:, cols] = acc


def _mm_res(ys, w, l, h, tm_pref=512):
    T, N = h.shape
    tm = _tile(T, tm_pref)
    y_specs, w_specs, row = [], [], 0
    for y in ys:
        k = y.shape[1]
        assert row % k == 0
        y_specs.append(pl.BlockSpec((tm, k), lambda i: (i, 0)))
        w_specs.append(_resident((None, k, N), functools.partial(lambda i, rb: (l, rb, 0), rb=row // k)))
        row += k
    assert row == w.shape[1]
    return pl.pallas_call(
        functools.partial(_mm_res_kernel, ny=len(ys)),
        grid=(T // tm,),
        in_specs=y_specs + w_specs + [pl.BlockSpec((tm, N), lambda i: (i, 0))],
        out_specs=pl.BlockSpec((tm, N), lambda i: (i, 0)),
        out_shape=jax.ShapeDtypeStruct((T, N), F32),
        compiler_params=_cparams(("parallel",)),
        name="mm_res",
    )(*ys, *([w] * len(ys)), h)


def _sgu_kernel(u_ref, v_ref, g_ref, ws_ref, bt_ref, o_ref, *, nchunk):
    C = CHUNK
    row = lax.broadcasted_iota(jnp.int32, (C, C), 0)
    col = lax.broadcasted_iota(jnp.int32, (C, C), 1)
    wm = [jnp.where(row >= col, ws_ref[g], 0.0).astype(BF16) for g in range(A_GROUPS)]
    bt = bt_ref[...]
    for c in range(nchunk):
        rows = slice(c * C, (c + 1) * C)
        u = jax.nn.gelu(u_ref[rows, :])
        v = _rms(jax.nn.gelu(v_ref[rows, :]), g_ref[...]).astype(BF16)
        for g in range(A_GROUPS):
            cols = slice(g * C, (g + 1) * C)
            z = _dot(wm[g], v[:, cols]) + bt[:, g:g + 1]
            o_ref[rows, cols] = (u[:, cols] * z).astype(o_ref.dtype)


def _sgu(proj, g, ws, bt, l, width):
    T = proj.shape[0]
    ts = _tile(T, 4 * CHUNK)
    return pl.pallas_call(
        functools.partial(_sgu_kernel, nchunk=ts // CHUNK),
        grid=(T // ts,),
        in_specs=[
            pl.BlockSpec((ts, width), lambda i: (i, 0)),
            pl.BlockSpec((ts, width), lambda i: (i, 1)),
            pl.BlockSpec((None, 1, width), lambda i: (l, 0, 0)),
            pl.BlockSpec((None, A_GROUPS, CHUNK, CHUNK), lambda i: (l, 0, 0, 0)),
            pl.BlockSpec((None, CHUNK, A_GROUPS), lambda i: (l, 0, 0)),
        ],
        out_specs=pl.BlockSpec((ts, width), lambda i: (i, 0)),
        out_shape=jax.ShapeDtypeStruct((T, width), BF16),
        compiler_params=_cparams(("parallel",)),
        name="sgu",
    )(proj, proj, g, ws, bt)


def _ret_kernel(q_ref, k_ref, v_ref, gate_ref, cos_ref, sin_ref, dmask_ref, zeta_ref, xi_ref, gch_ref, gn_ref,
                o_ref, state_ref, *, nchunk):
    C = CHUNK
    hd = C
    half = hd // 2

    @pl.when(pl.program_id(1) == 0)
    def _():
        state_ref[...] = jnp.zeros_like(state_ref)

    for c in range(nchunk):
        rows = slice(c * C, (c + 1) * C)
        cos = cos_ref[rows, :]
        sin = sin_ref[rows, :]
        for h in range(R_HEADS):
            cols = slice(h * hd, (h + 1) * hd)
            q = q_ref[rows, cols]
            k = k_ref[rows, cols]
            q = q * cos + pltpu.roll(q, half, 1) * sin
            k = (k * cos + pltpu.roll(k, half, 1) * sin) * (hd ** -0.5)
            vb = v_ref[rows, cols].astype(BF16)
            scores = _dot_nt(q.astype(BF16), k.astype(BF16)) * dmask_ref[h]
            state = state_ref[h]
            y = _dot(scores.astype(BF16), vb) + _dot((q * xi_ref[:, cols]).astype(BF16), state.astype(BF16))
            kz = (k * zeta_ref[:, cols]).astype(BF16)
            state_ref[h] = gch_ref[h] * state + _dot(kz.T, vb)
            mu = jnp.mean(y, axis=-1, keepdims=True)
            yc = y - mu
            var = jnp.mean(yc * yc, axis=-1, keepdims=True)
            yn = yc * lax.rsqrt(var + EPS) * gn_ref[:, cols]
            o_ref[rows, cols] = (_silu(gate_ref[rows, cols]) * yn).astype(o_ref.dtype)


def _retention(proj, cos, sin, dmask, zeta, xi, gch, gn, l, B, S, width, col0):
    T = proj.shape[0]
    tr = _tile(S, 4 * CHUNK)
    ns = S // tr
    cb = col0 // width
    assert col0 % width == 0
    tok = lambda b, n: b * ns + n
    return pl.pallas_call(
        functools.partial(_ret_kernel, nchunk=tr // CHUNK),
        grid=(B, ns),
        in_specs=[
            pl.BlockSpec((tr, width), lambda b, n: (tok(b, n), cb)),
            pl.BlockSpec((tr, width), lambda b, n: (tok(b, n), cb + 1)),
            pl.BlockSpec((tr, width), lambda b, n: (tok(b, n), cb + 2)),
            pl.BlockSpec((tr, width), lambda b, n: (tok(b, n), cb + 3)),
            pl.BlockSpec((tr, CHUNK), lambda b, n: (n, 0)),
            pl.BlockSpec((tr, CHUNK), lambda b, n: (n, 0)),
            pl.BlockSpec((R_HEADS, CHUNK, CHUNK), lambda b, n: (0, 0, 0)),
            pl.BlockSpec((CHUNK, width), lambda b, n: (0, 0)),
            pl.BlockSpec((CHUNK, width), lambda b, n: (0, 0)),
            pl.BlockSpec((R_HEADS, 1, CHUNK), lambda b, n: (0, 0, 0)),
            pl.BlockSpec((None, 1, width), lambda b, n: (l, 0, 0)),
        ],
        out_specs=pl.BlockSpec((tr, width), lambda b, n: (tok(b, n), 0)),
        out_shape=jax.ShapeDtypeStruct((T, width), BF16),
        scratch_shapes=[pltpu.VMEM((R_HEADS, CHUNK, CHUNK), F32)],
        compiler_params=_cparams(("parallel", "arbitrary")),
        name="retention",
    )(proj, proj, proj, proj, cos, sin, dmask, zeta, xi, gch, gn)


def _retention_tables(S):
    C, H, dim = CHUNK, R_HEADS, CHUNK
    half = dim // 2
    pos = jnp.arange(S, dtype=F32)
    inv_freq = ROPE_BASE ** (-jnp.arange(half, dtype=F32) * 2.0 / dim)
    ang = pos[:, None] * inv_freq[None, :]
    cos, sin = jnp.cos(ang), jnp.sin(ang)
    cos_t = jnp.concatenate([cos, cos], axis=-1)
    sin_t = jnp.concatenate([-sin, sin], axis=-1)
    log_g = jnp.log1p(-jnp.exp2(-5.0 - jnp.arange(H, dtype=F32)))
    i = jnp.arange(C, dtype=F32)
    diff = i[:, None] - i[None, :]
    dmask = jnp.where(diff[None] >= 0, jnp.exp(jnp.maximum(diff, 0.0)[None] * log_g[:, None, None]), 0.0)
    zeta = jnp.exp((C - 1 - i)[None, :] * log_g[:, None]).T
    xi = jnp.exp((i + 1)[None, :] * log_g[:, None]).T
    gch = jnp.exp(C * log_g)
    rep = lambda t: jnp.repeat(t, dim, axis=1)
    return cos_t, sin_t, dmask, rep(zeta), rep(xi), jnp.broadcast_to(gch[:, None, None], (H, 1, C))


def _mla_proj_kernel(cq_ref, ckv_ref, kr_ref, qg_ref, kvg_ref, wq_ref, wkv_ref, rt_ref, q_ref, k_ref, v_ref):
    hq = MLA_NOPE + 2 * MLA_ROPE
    scale = (MLA_NOPE + MLA_ROPE) ** -0.5 * math.log2(math.e)
    lane = lax.broadcasted_iota(jnp.int32, (1, 2 * MLA_ROPE), 1)
    keep = jnp.where(lane < MLA_ROPE, scale, 0.0)
    rt = rt_ref[...]

    def rope(y):
        z = y * rt
        return z + pltpu.roll(z, MLA_ROPE, 1)

    kr = rope(kr_ref[...]).astype(BF16)
    q = _dot(_rms(cq_ref[...], qg_ref[...]).astype(BF16), wq_ref[...])
    kv = _dot(_rms(ckv_ref[...], kvg_ref[...]).astype(BF16), wkv_ref[...])
    for h in range(MLA_HEADS):
        q_ref[:, h * hq:h * hq + MLA_NOPE] = (q[:, h * hq:h * hq + MLA_NOPE] * scale).astype(BF16)
        q_ref[:, h * hq + MLA_NOPE:(h + 1) * hq] = (rope(q[:, h * hq + MLA_NOPE:(h + 1) * hq]) * keep).astype(BF16)
        k_ref[:, h * hq:h * hq + MLA_NOPE] = kv[:, h * MLA_NOPE:(h + 1) * MLA_NOPE].astype(BF16)
        k_ref[:, h * hq + MLA_NOPE:(h + 1) * hq] = kr
        v0 = MLA_HEADS * MLA_NOPE + h * MLA_V
        v_ref[:, 2 * h * MLA_V:(2 * h + 1) * MLA_V] = kv[:, v0:v0 + MLA_V].astype(BF16)
        v_ref[:, (2 * h + 1) * MLA_V:(2 * h + 2) * MLA_V] = jnp.ones((kv.shape[0], MLA_V), BF16)


def _mla_proj(proj, qg, kvg, wq, wkv, rt, l, S, q_lora, kv_lora, col0):
    T = proj.shape[0]
    tm = _tile(S, 512)
    ns = S // tm
    hq = MLA_NOPE + 2 * MLA_ROPE
    c_kv0 = col0 + q_lora
    c_kr0 = c_kv0 + kv_lora
    assert col0 % q_lora == 0 and c_kv0 % kv_lora == 0 and c_kr0 % LANES == 0
    return pl.pallas_call(
        _mla_proj_kernel,
        grid=(T // tm,),
        in_specs=[
            pl.BlockSpec((tm, q_lora), lambda i: (i, col0 // q_lora)),
            pl.BlockSpec((tm, kv_lora), lambda i: (i, c_kv0 // kv_lora)),
            pl.BlockSpec((tm, 2 * MLA_ROPE), lambda i: (i, c_kr0 // LANES)),
            pl.BlockSpec((None, 1, q_lora), lambda i: (l, 0, 0)),
            pl.BlockSpec((None, 1, kv_lora), lambda i: (l, 0, 0)),
            pl.BlockSpec((None, q_lora, MLA_HEADS * hq), lambda i: (l, 0, 0)),
            pl.BlockSpec((None, kv_lora, MLA_HEADS * (MLA_NOPE + MLA_V)), lambda i: (l, 0, 0)),
            pl.BlockSpec((tm, 2 * MLA_ROPE), lambda i: (i % ns, 0)),
        ],
        out_specs=[
            pl.BlockSpec((tm, MLA_HEADS * hq), lambda i: (i, 0)),
            pl.BlockSpec((tm, MLA_HEADS * hq), lambda i: (i, 0)),
            pl.BlockSpec((tm, MLA_HEADS * 2 * MLA_V), lambda i: (i, 0)),
        ],
        out_shape=[
            jax.ShapeDtypeStruct((T, MLA_HEADS * hq), BF16),
            jax.ShapeDtypeStruct((T, MLA_HEADS * hq), BF16),
            jax.ShapeDtypeStruct((T, MLA_HEADS * 2 * MLA_V), BF16),
        ],
        compiler_params=_cparams(("parallel",)),
        name="mla_proj",
    )(proj, proj, proj, qg, kvg, wq, wkv, rt)


def _flash_kernel(q_ref, k_ref, v_ref, o_ref, s_ref, mc_ref, m_ref, acc_ref, *, tq):
    qi = pl.program_id(2)
    rep = tq // LANES
    q = q_ref[...]

    def keys(j):
        return pl.ds(pl.multiple_of(j * tq, tq), tq)

    def rowmax(s):
        return jnp.broadcast_to(jnp.max(s, axis=-1, keepdims=True), (tq, LANES))

    def scores(j, slot):
        s = _dot_nt(q, k_ref[keys(j), :])
        s_ref[slot] = s
        mc_ref[slot] = rowmax(s)

    def update(j, slot, diagonal=False):
        s = s_ref[slot]
        if diagonal:
            row = lax.broadcasted_iota(jnp.int32, s.shape, 0)
            col = lax.broadcasted_iota(jnp.int32, s.shape, 1)
            s = jnp.where(col <= row, s, -1e30)
            m_cur = rowmax(s)
        else:
            m_cur = mc_ref[slot]
        m = m_ref[...]
        m_new = jnp.maximum(m, m_cur)
        m_ref[...] = m_new
        p = jnp.exp2(s - jnp.tile(m_new, (1, rep)))
        alpha = jnp.exp2(m - m_new)
        acc_ref[...] = jnp.tile(alpha, (1, 2)) * acc_ref[...] + _dot(p.astype(BF16), v_ref[keys(j), :])

    def finish():
        o_ref[...] = (acc_ref[:, :MLA_V] / acc_ref[:, MLA_V:]).astype(o_ref.dtype)

    m_ref[...] = jnp.full(m_ref.shape, -1e30, F32)
    acc_ref[...] = jnp.zeros(acc_ref.shape, F32)
    scores(0, 0)

    @pl.loop(0, qi // 2)
    def _(t):
        scores(2 * t + 1, 1)
        update(2 * t, 0)
        scores(2 * t + 2, 0)
        update(2 * t + 1, 1)

    @pl.when(qi % 2 == 1)
    def _():
        scores(qi, 1)
        update(qi - 1, 0)
        update(qi, 1, diagonal=True)
        finish()

    @pl.when(qi % 2 == 0)
    def _():
        update(qi, 0, diagonal=True)
        finish()


def _flash(q, k, v, B, S):
    T = q.shape[0]
    tq = _tile(S, 512)
    nq = S // tq
    hq = MLA_NOPE + 2 * MLA_ROPE
    return pl.pallas_call(
        functools.partial(_flash_kernel, tq=tq),
        grid=(B, MLA_HEADS, nq),
        in_specs=[
            pl.BlockSpec((tq, hq), lambda b, h, i: (b * nq + i, h)),
            pl.BlockSpec((S, hq), lambda b, h, i: (b, h)),
            pl.BlockSpec((S, 2 * MLA_V), lambda b, h, i: (b, h)),
        ],
        out_specs=pl.BlockSpec((tq, MLA_V), lambda b, h, i: (b * nq + i, h)),
        out_shape=jax.ShapeDtypeStruct((T, MLA_HEADS * MLA_V), BF16),
        scratch_shapes=[
            pltpu.VMEM((2, tq, tq), F32),
            pltpu.VMEM((2, tq, LANES), F32),
            pltpu.VMEM((tq, LANES), F32),
            pltpu.VMEM((tq, 2 * MLA_V), F32),
        ],
        compiler_params=_cparams(("parallel", "parallel", "arbitrary")),
        name="mla_flash",
    )(q, k, v)


def _mla_rope_table(S):
    half = MLA_ROPE // 2
    pos = jnp.arange(S, dtype=F32)
    inv_freq = ROPE_BASE ** (-jnp.arange(half, dtype=F32) * 2.0 / MLA_ROPE)
    ang = pos[:, None] * inv_freq[None, :]
    cos, sin = jnp.cos(ang), jnp.sin(ang)
    return jnp.concatenate([cos, cos, -sin, sin], axis=-1)


def _swap_halves(w):
    half = w.shape[-1] // 2
    return jnp.concatenate([w[..., half:], w[..., :half]], axis=-1)


def _xattn_kernel(q_ref, k_ref, v_ref, o_ref, s_ref, *, hd):
    scale = hd ** -0.5
    heads = [slice(h * hd, (h + 1) * hd) for h in range(XA_HEADS)]
    for h, cols in enumerate(heads):
        s_ref[h] = _dot_nt(q_ref[:, cols], k_ref[:, cols]) * scale
    for h, cols in enumerate(heads):
        s = s_ref[h]
        e = jnp.exp(s - jnp.max(s, axis=-1, keepdims=True))
        p = e / jnp.sum(e, axis=-1, keepdims=True)
        o_ref[:, cols] = _dot(p.astype(BF16), v_ref[:, cols]).astype(o_ref.dtype)


def _xattn(q, kv, S, M):
    T, D = q.shape
    tm = _tile(S, 512)
    ns = S // tm
    return pl.pallas_call(
        functools.partial(_xattn_kernel, hd=D // XA_HEADS),
        grid=(T // tm,),
        in_specs=[
            pl.BlockSpec((tm, D), lambda i: (i, 0)),
            pl.BlockSpec((M, D), lambda i: (i // ns, 0)),
            pl.BlockSpec((M, D), lambda i: (i // ns, 1)),
        ],
        out_specs=pl.BlockSpec((tm, D), lambda i: (i, 0)),
        out_shape=jax.ShapeDtypeStruct((T, D), BF16),
        scratch_shapes=[pltpu.VMEM((XA_HEADS, tm, M), F32)],
        compiler_params=_cparams(("parallel",)),
        name="xattn",
    )(q, kv, kv)


def kernel(x, mem, ffn1_norm, ffn1_w_gate, ffn1_w_up, ffn1_w_down, mix_norm, w_in, sgu_norm, sgu_w_s, sgu_b, ret_gn, q_norm, w_uq, kv_norm, w_ukv, w_out, xa_norm, mem_norm, xa_wq, xa_wkv, xa_wo, ffn2_norm, ffn2_w_gate, ffn2_w_up, ffn2_w_down, final_norm):
    B, S, D = x.shape
    M = mem.shape[1]
    L = w_in.shape[0]
    T = B * S
    a_width = sgu_norm.shape[-1]
    r_width = ret_gn.shape[-1]
    q_lora = q_norm.shape[-1]
    kv_lora = kv_norm.shape[-1]
    assert a_width == A_GROUPS * CHUNK and r_width == R_HEADS * CHUNK and S % CHUNK == 0
    ret_col0 = 2 * a_width
    mla_col0 = ret_col0 + 4 * r_width
    kr_col0 = mla_col0 + q_lora + kv_lora
    assert w_in.shape[-1] == kr_col0 + MLA_ROPE

    bf = lambda w: w.astype(BF16)
    row = lambda g: g.reshape(g.shape[0], 1, g.shape[1])

    n_in = kr_col0 + 2 * MLA_ROPE
    n_in_pad = -(-n_in // 1024) * 1024
    w_in_x = jnp.concatenate(
        [bf(w_in), bf(_swap_halves(w_in[..., kr_col0:])), jnp.zeros((L, D, n_in_pad - n_in), BF16)], axis=-1)
    wq4 = w_uq.reshape(L, q_lora, MLA_HEADS, MLA_NOPE + MLA_ROPE)
    wq_x = bf(jnp.concatenate([wq4, _swap_halves(wq4[..., MLA_NOPE:])], axis=-1)).reshape(L, q_lora, -1)
    wkv4 = w_ukv.reshape(L, kv_lora, MLA_HEADS, MLA_NOPE + MLA_V)
    wkv_x = bf(jnp.concatenate([wkv4[..., :MLA_NOPE].reshape(L, kv_lora, -1),
                                wkv4[..., MLA_NOPE:].reshape(L, kv_lora, -1)], axis=-1))
    sgu_bt = jnp.swapaxes(sgu_b, 1, 2)
    ret_tabs = _retention_tables(S)
    mla_rt = _mla_rope_table(S)
    w1g, w1u, w1d = bf(ffn1_w_gate), bf(ffn1_w_up), bf(ffn1_w_down)
    w2g, w2u, w2d = bf(ffn2_w_gate), bf(ffn2_w_up), bf(ffn2_w_down)
    w_out_b, xa_wq_b, xa_wkv_b, xa_wo_b = bf(w_out), bf(xa_wq), bf(xa_wkv), bf(xa_wo)
    fin = final_norm.reshape(1, D)

    h = x.reshape(T, D)
    mem2 = mem.reshape(B * M, D)
    for l in range(L):
        h = _ffn(h, row(ffn1_norm), w1g, w1u, w1d, l, fin, False)
        proj = _norm_mm(h, row(mix_norm), w_in_x, l, F32)
        y_a = _sgu(proj, row(sgu_norm), sgu_w_s, sgu_bt, l, a_width)
        y_r = _retention(proj, *ret_tabs, row(ret_gn), l, B, S, r_width, ret_col0)
        q, k, v = _mla_proj(proj, row(q_norm), row(kv_norm), wq_x, wkv_x, mla_rt, l, S, q_lora, kv_lora, mla_col0)
        y_c = _flash(q, k, v, B, S)
        h = _mm_res([y_a, y_r, y_c], w_out_b, l, h)
        kvm = _norm_mm(mem2, row(mem_norm), xa_wkv_b, l, BF16)
        xq = _norm_mm(h, row(xa_norm), xa_wq_b, l, BF16)
        o = _xattn(xq, kvm, S, M)
        h = _mm_res([o], xa_wo_b, l, h)
        h = _ffn(h, row(ffn2_norm), w2g, w2u, w2d, l, fin, l == L - 1)
    return h.reshape(B, S, D)
```

```python
import functools
import math

import jax
import jax.numpy as jnp
from jax import lax
from jax.experimental import pallas as pl
from jax.experimental.pallas import tpu as pltpu

F32 = jnp.float32
BF16 = jnp.bfloat16

EPS = 1e-6
ROPE_BASE = 10000.0
CHUNK = 128
A_GROUPS = 4
R_HEADS = 4
MLA_HEADS = 8
MLA_NOPE = 128
MLA_ROPE = 64
MLA_V = 128
XA_HEADS = 4
LANES = 128

VMEM_LIMIT = 56 * 1024 * 1024
FFN_VMEM_LIMIT = 62 * 1024 * 1024


def _cparams(sem, vmem_limit=VMEM_LIMIT):
    return pltpu.CompilerParams(dimension_semantics=sem, vmem_limit_bytes=vmem_limit)


def _tile(n, pref):
    t = min(n, pref)
    assert n % t == 0, (n, t)
    return t


def _rms(x, g):
    return x * lax.rsqrt(jnp.mean(x * x, axis=-1, keepdims=True) + EPS) * g


def _silu(x):
    return x * (1.0 / (1.0 + jnp.exp(-x)))


def _dot(a, b):
    return jnp.dot(a, b, preferred_element_type=F32)


def _dot_nt(a, b):
    return lax.dot_general(a, b, (((1,), (1,)), ((), ())), preferred_element_type=F32)


COL_CHUNK = 1024
ROW_CHUNK = 256


def _ffn_kernel(h_ref, g_ref, wg_ref, wu_ref, wd_ref, fg_ref, o_ref, n_ref, *, nf, final):
    f = pl.program_id(1)
    tm, D = o_ref.shape
    row_chunks = [slice(r, r + ROW_CHUNK) for r in range(0, tm, ROW_CHUNK)]

    def step(first, last):
        if first:
            n = jnp.concatenate([_rms(h_ref[rows, :], g_ref[...]).astype(BF16) for rows in row_chunks], axis=0)
            n_ref[...] = n
        else:
            n = n_ref[...]
        a = _dot(n, wg_ref[...])
        b = _dot(n, wu_ref[...])
        hm = (_silu(a) * b).astype(BF16)
        for c in range(0, D, COL_CHUNK):
            cols = slice(c, c + COL_CHUNK)
            acc = _dot(hm, wd_ref[:, cols])
            if not first:
                acc = o_ref[:, cols] + acc
            o_ref[:, cols] = h_ref[:, cols] + 0.5 * acc if last else acc
        if last and final:
            for rows in row_chunks:
                o_ref[rows, :] = _rms(o_ref[rows, :], fg_ref[...])

    assert nf > 2
    pl.when(f == 0)(functools.partial(step, True, False))
    pl.when((f > 0) & (f < nf - 1))(functools.partial(step, False, False))
    pl.when(f == nf - 1)(functools.partial(step, False, True))


def _ffn(h, g, wg, wu, wd, l, final_g, final):
    T, D = h.shape
    F = wg.shape[-1]
    tm = _tile(T, 1024)
    tf = _tile(F, 512)
    nf = F // tf
    return pl.pallas_call(
        functools.partial(_ffn_kernel, nf=nf, final=final),
        grid=(T // tm, nf),
        in_specs=[
            pl.BlockSpec((tm, D), lambda i, f: (i, 0)),
            pl.BlockSpec((None, 1, D), lambda i, f: (l, 0, 0)),
            pl.BlockSpec((None, D, tf), lambda i, f: (l, 0, f)),
            pl.BlockSpec((None, D, tf), lambda i, f: (l, 0, f)),
            pl.BlockSpec((None, tf, D), lambda i, f: (l, f, 0)),
            pl.BlockSpec((1, D), lambda i, f: (0, 0)),
        ],
        out_specs=pl.BlockSpec((tm, D), lambda i, f: (i, 0)),
        out_shape=jax.ShapeDtypeStruct((T, D), F32),
        scratch_shapes=[pltpu.VMEM((tm, D), BF16)],
        compiler_params=_cparams(("parallel", "arbitrary"), FFN_VMEM_LIMIT),
        name="ffn",
    )(h, g, wg, wu, wd, final_g)


def _resident(block_shape, index_map):
    return pl.BlockSpec(block_shape, index_map, pipeline_mode=pl.Buffered(1))


def _norm_mm_kernel(x_ref, g_ref, w_ref, o_ref):
    n = _rms(x_ref[...], g_ref[...]).astype(BF16)
    N = o_ref.shape[1]
    tn = min(N, COL_CHUNK)
    for c in range(N // tn):
        o_ref[:, c * tn:(c + 1) * tn] = _dot(n, w_ref[:, c * tn:(c + 1) * tn]).astype(o_ref.dtype)


def _norm_mm(x, g, w, l, out_dtype, tm_pref=512):
    M, K = x.shape
    N = w.shape[-1]
    tm = _tile(M, tm_pref)
    return pl.pallas_call(
        _norm_mm_kernel,
        grid=(M // tm,),
        in_specs=[
            pl.BlockSpec((tm, K), lambda i: (i, 0)),
            _resident((None, 1, K), lambda i: (l, 0, 0)),
            _resident((None, K, N), lambda i: (l, 0, 0)),
        ],
        out_specs=pl.BlockSpec((tm, N), lambda i: (i, 0)),
        out_shape=jax.ShapeDtypeStruct((M, N), out_dtype),
        compiler_params=_cparams(("parallel",)),
        name="norm_mm",
    )(x, g, w)


def _mm_res_kernel(*refs, ny):
    ys, ws, h_ref, o_ref = refs[:ny], refs[ny:2 * ny], refs[2 * ny], refs[2 * ny + 1]
    N = o_ref.shape[1]
    tn = min(N, COL_CHUNK)
    for c in range(N // tn):
        cols = slice(c * tn, (c + 1) * tn)
        acc = h_ref[:, cols]
        for y_ref, w_ref in zip(ys, ws):
            acc = acc + _dot(y_ref[...], w_ref[:, cols])
        o_ref[:, cols] = acc


def _mm_res(ys, w, l, h, tm_pref=512):
    T, N = h.shape
    tm = _tile(T, tm_pref)
    y_specs, w_specs, row = [], [], 0
    for y in ys:
        k = y.shape[1]
        assert row % k == 0
        y_specs.append(pl.BlockSpec((tm, k), lambda i: (i, 0)))
        w_specs.append(_resident((None, k, N), functools.partial(lambda i, rb: (l, rb, 0), rb=row // k)))
        row += k
    assert row == w.shape[1]
    return pl.pallas_call(
        functools.partial(_mm_res_kernel, ny=len(ys)),
        grid=(T // tm,),
        in_specs=y_specs + w_specs + [pl.BlockSpec((tm, N), lambda i: (i, 0))],
        out_specs=pl.BlockSpec((tm, N), lambda i: (i, 0)),
        out_shape=jax.ShapeDtypeStruct((T, N), F32),
        compiler_params=_cparams(("parallel",)),
        name="mm_res",
    )(*ys, *([w] * len(ys)), h)


def _sgu_kernel(u_ref, v_ref, g_ref, ws_ref, bt_ref, o_ref, *, nchunk):
    C = CHUNK
    row = lax.broadcasted_iota(jnp.int32, (C, C), 0)
    col = lax.broadcasted_iota(jnp.int32, (C, C), 1)
    wm = [jnp.where(row >= col, ws_ref[g], 0.0).astype(BF16) for g in range(A_GROUPS)]
    bt = bt_ref[...]
    for c in range(nchunk):
        rows = slice(c * C, (c + 1) * C)
        u = jax.nn.gelu(u_ref[rows, :])
        v = _rms(jax.nn.gelu(v_ref[rows, :]), g_ref[...]).astype(BF16)
        for g in range(A_GROUPS):
            cols = slice(g * C, (g + 1) * C)
            z = _dot(wm[g], v[:, cols]) + bt[:, g:g + 1]
            o_ref[rows, cols] = (u[:, cols] * z).astype(o_ref.dtype)


def _sgu(proj, g, ws, bt, l, width):
    T = proj.shape[0]
    ts = _tile(T, 4 * CHUNK)
    return pl.pallas_call(
        functools.partial(_sgu_kernel, nchunk=ts // CHUNK),
        grid=(T // ts,),
        in_specs=[
            pl.BlockSpec((ts, width), lambda i: (i, 0)),
            pl.BlockSpec((ts, width), lambda i: (i, 1)),
            pl.BlockSpec((None, 1, width), lambda i: (l, 0, 0)),
            pl.BlockSpec((None, A_GROUPS, CHUNK, CHUNK), lambda i: (l, 0, 0, 0)),
            pl.BlockSpec((None, CHUNK, A_GROUPS), lambda i: (l, 0, 0)),
        ],
        out_specs=pl.BlockSpec((ts, width), lambda i: (i, 0)),
        out_shape=jax.ShapeDtypeStruct((T, width), BF16),
        compiler_params=_cparams(("parallel",)),
        name="sgu",
    )(proj, proj, g, ws, bt)


def _ret_kernel(q_ref, k_ref, v_ref, gate_ref, cos_ref, sin_ref, dmask_ref, zeta_ref, xi_ref, gch_ref, gn_ref,
                o_ref, state_ref, *, nchunk):
    C = CHUNK
    hd = C
    half = hd // 2

    @pl.when(pl.program_id(1) == 0)
    def _():
        state_ref[...] = jnp.zeros_like(state_ref)

    for c in range(nchunk):
        rows = slice(c * C, (c + 1) * C)
        cos = cos_ref[rows, :]
        sin = sin_ref[rows, :]
        for h in range(R_HEADS):
            cols = slice(h * hd, (h + 1) * hd)
            q = q_ref[rows, cols]
            k = k_ref[rows, cols]
            q = q * cos + pltpu.roll(q, half, 1) * sin
            k = (k * cos + pltpu.roll(k, half, 1) * sin) * (hd ** -0.5)
            vb = v_ref[rows, cols].astype(BF16)
            scores = _dot_nt(q.astype(BF16), k.astype(BF16)) * dmask_ref[h]
            state = state_ref[h]
            y = _dot(scores.astype(BF16), vb) + _dot((q * xi_ref[:, cols]).astype(BF16), state.astype(BF16))
            kz = (k * zeta_ref[:, cols]).astype(BF16)
            state_ref[h] = gch_ref[h] * state + _dot(kz.T, vb)
            mu = jnp.mean(y, axis=-1, keepdims=True)
            yc = y - mu
            var = jnp.mean(yc * yc, axis=-1, keepdims=True)
            yn = yc * lax.rsqrt(var + EPS) * gn_ref[:, cols]
            o_ref[rows, cols] = (_silu(gate_ref[rows, cols]) * yn).astype(o_ref.dtype)


def _retention(proj, cos, sin, dmask, zeta, xi, gch, gn, l, B, S, width, col0):
    T = proj.shape[0]
    tr = _tile(S, 4 * CHUNK)
    ns = S // tr
    cb = col0 // width
    assert col0 % width == 0
    tok = lambda b, n: b * ns + n
    return pl.pallas_call(
        functools.partial(_ret_kernel, nchunk=tr // CHUNK),
        grid=(B, ns),
        in_specs=[
            pl.BlockSpec((tr, width), lambda b, n: (tok(b, n), cb)),
            pl.BlockSpec((tr, width), lambda b, n: (tok(b, n), cb + 1)),
            pl.BlockSpec((tr, width), lambda b, n: (tok(b, n), cb + 2)),
            pl.BlockSpec((tr, width), lambda b, n: (tok(b, n), cb + 3)),
            pl.BlockSpec((tr, CHUNK), lambda b, n: (n, 0)),
            pl.BlockSpec((tr, CHUNK), lambda b, n: (n, 0)),
            pl.BlockSpec((R_HEADS, CHUNK, CHUNK), lambda b, n: (0, 0, 0)),
            pl.BlockSpec((CHUNK, width), lambda b, n: (0, 0)),
            pl.BlockSpec((CHUNK, width), lambda b, n: (0, 0)),
            pl.BlockSpec((R_HEADS, 1, CHUNK), lambda b, n: (0, 0, 0)),
            pl.BlockSpec((None, 1, width), lambda b, n: (l, 0, 0)),
        ],
        out_specs=pl.BlockSpec((tr, width), lambda b, n: (tok(b, n), 0)),
        out_shape=jax.ShapeDtypeStruct((T, width), BF16),
        scratch_shapes=[pltpu.VMEM((R_HEADS, CHUNK, CHUNK), F32)],
        compiler_params=_cparams(("parallel", "arbitrary")),
        name="retention",
    )(proj, proj, proj, proj, cos, sin, dmask, zeta, xi, gch, gn)


def _retention_tables(S):
    C, H, dim = CHUNK, R_HEADS, CHUNK
    half = dim // 2
    pos = jnp.arange(S, dtype=F32)
    inv_freq = ROPE_BASE ** (-jnp.arange(half, dtype=F32) * 2.0 / dim)
    ang = pos[:, None] * inv_freq[None, :]
    cos, sin = jnp.cos(ang), jnp.sin(ang)
    cos_t = jnp.concatenate([cos, cos], axis=-1)
    sin_t = jnp.concatenate([-sin, sin], axis=-1)
    log_g = jnp.log1p(-jnp.exp2(-5.0 - jnp.arange(H, dtype=F32)))
    i = jnp.arange(C, dtype=F32)
    diff = i[:, None] - i[None, :]
    dmask = jnp.where(diff[None] >= 0, jnp.exp(jnp.maximum(diff, 0.0)[None] * log_g[:, None, None]), 0.0)
    zeta = jnp.exp((C - 1 - i)[None, :] * log_g[:, None]).T
    xi = jnp.exp((i + 1)[None, :] * log_g[:, None]).T
    gch = jnp.exp(C * log_g)
    rep = lambda t: jnp.repeat(t, dim, axis=1)
    return cos_t, sin_t, dmask, rep(zeta), rep(xi), jnp.broadcast_to(gch[:, None, None], (H, 1, C))


def _mla_proj_kernel(cq_ref, ckv_ref, kr_ref, qg_ref, kvg_ref, wq_ref, wkv_ref, rt_ref, q_ref, k_ref, v_ref):
    hq = MLA_NOPE + 2 * MLA_ROPE
    scale = (MLA_NOPE + MLA_ROPE) ** -0.5 * math.log2(math.e)
    lane = lax.broadcasted_iota(jnp.int32, (1, 2 * MLA_ROPE), 1)
    keep = jnp.where(lane < MLA_ROPE, scale, 0.0)
    rt = rt_ref[...]

    def rope(y):
        z = y * rt
        return z + pltpu.roll(z, MLA_ROPE, 1)

    x = kr_ref[...]
    half = MLA_ROPE // 2
    swapped = jnp.where(lane < half, pltpu.roll(x, 2 * MLA_ROPE - half, 1), pltpu.roll(x, half, 1))
    kr = jnp.where(lane < MLA_ROPE, x * rt + swapped * pltpu.roll(rt, MLA_ROPE, 1), 0.0).astype(BF16)
    q = _dot(_rms(cq_ref[...], qg_ref[...]).astype(BF16), wq_ref[...])
    kv = _dot(_rms(ckv_ref[...], kvg_ref[...]).astype(BF16), wkv_ref[...])
    for h in range(MLA_HEADS):
        q_ref[:, h * hq:h * hq + MLA_NOPE] = (q[:, h * hq:h * hq + MLA_NOPE] * scale).astype(BF16)
        q_ref[:, h * hq + MLA_NOPE:(h + 1) * hq] = (rope(q[:, h * hq + MLA_NOPE:(h + 1) * hq]) * keep).astype(BF16)
        k_ref[:, h * hq:h * hq + MLA_NOPE] = kv[:, h * MLA_NOPE:(h + 1) * MLA_NOPE].astype(BF16)
        k_ref[:, h * hq + MLA_NOPE:(h + 1) * hq] = kr
        v0 = MLA_HEADS * MLA_NOPE + h * MLA_V
        v_ref[:, 2 * h * MLA_V:(2 * h + 1) * MLA_V] = kv[:, v0:v0 + MLA_V].astype(BF16)
        v_ref[:, (2 * h + 1) * MLA_V:(2 * h + 2) * MLA_V] = jnp.ones((kv.shape[0], MLA_V), BF16)


def _mla_proj(proj, qg, kvg, wq, wkv, rt, l, S, q_lora, kv_lora, col0):
    T = proj.shape[0]
    tm = _tile(S, 512)
    ns = S // tm
    hq = MLA_NOPE + 2 * MLA_ROPE
    c_kv0 = col0 + q_lora
    c_kr0 = c_kv0 + kv_lora
    assert col0 % q_lora == 0 and c_kv0 % kv_lora == 0 and c_kr0 % LANES == 0
    return pl.pallas_call(
        _mla_proj_kernel,
        grid=(T // tm,),
        in_specs=[
            pl.BlockSpec((tm, q_lora), lambda i: (i, col0 // q_lora)),
            pl.BlockSpec((tm, kv_lora), lambda i: (i, c_kv0 // kv_lora)),
            pl.BlockSpec((tm, 2 * MLA_ROPE), lambda i: (i, c_kr0 // LANES)),
            pl.BlockSpec((None, 1, q_lora), lambda i: (l, 0, 0)),
            pl.BlockSpec((None, 1, kv_lora), lambda i: (l, 0, 0)),
            pl.BlockSpec((None, q_lora, MLA_HEADS * hq), lambda i: (l, 0, 0)),
            pl.BlockSpec((None, kv_lora, MLA_HEADS * (MLA_NOPE + MLA_V)), lambda i: (l, 0, 0)),
            pl.BlockSpec((tm, 2 * MLA_ROPE), lambda i: (i % ns, 0)),
        ],
        out_specs=[
            pl.BlockSpec((tm, MLA_HEADS * hq), lambda i: (i, 0)),
            pl.BlockSpec((tm, MLA_HEADS * hq), lambda i: (i, 0)),
            pl.BlockSpec((tm, MLA_HEADS * 2 * MLA_V), lambda i: (i, 0)),
        ],
        out_shape=[
            jax.ShapeDtypeStruct((T, MLA_HEADS * hq), BF16),
            jax.ShapeDtypeStruct((T, MLA_HEADS * hq), BF16),
            jax.ShapeDtypeStruct((T, MLA_HEADS * 2 * MLA_V), BF16),
        ],
        compiler_params=_cparams(("parallel",)),
        name="mla_proj",
    )(proj, proj, proj, qg, kvg, wq, wkv, rt)


def _flash_kernel(q_ref, k_ref, v_ref, o_ref, s_ref, mc_ref, m_ref, acc_ref, *, tq):
    qi = pl.program_id(2)
    rep = tq // LANES
    q = q_ref[...]

    def keys(j):
        return pl.ds(pl.multiple_of(j * tq, tq), tq)

    def rowmax(s):
        return jnp.broadcast_to(jnp.max(s, axis=-1, keepdims=True), (tq, LANES))

    def scores(j, slot):
        s = _dot_nt(q, k_ref[keys(j), :])
        s_ref[slot] = s
        mc_ref[slot] = rowmax(s)

    def update(j, slot, diagonal=False):
        s = s_ref[slot]
        if diagonal:
            row = lax.broadcasted_iota(jnp.int32, s.shape, 0)
            col = lax.broadcasted_iota(jnp.int32, s.shape, 1)
            s = jnp.where(col <= row, s, -1e30)
            m_cur = rowmax(s)
        else:
            m_cur = mc_ref[slot]
        m = m_ref[...]
        m_new = jnp.maximum(m, m_cur)
        m_ref[...] = m_new
        p = jnp.exp2(s - jnp.tile(m_new, (1, rep)))
        alpha = jnp.exp2(m - m_new)
        acc_ref[...] = jnp.tile(alpha, (1, 2)) * acc_ref[...] + _dot(p.astype(BF16), v_ref[keys(j), :])

    def finish():
        o_ref[...] = (acc_ref[:, :MLA_V] / acc_ref[:, MLA_V:]).astype(o_ref.dtype)

    m_ref[...] = jnp.full(m_ref.shape, -1e30, F32)
    acc_ref[...] = jnp.zeros(acc_ref.shape, F32)
    scores(0, 0)

    @pl.loop(0, qi // 2)
    def _(t):
        scores(2 * t + 1, 1)
        update(2 * t, 0)
        scores(2 * t + 2, 0)
        update(2 * t + 1, 1)

    @pl.when(qi % 2 == 1)
    def _():
        scores(qi, 1)
        update(qi - 1, 0)
        update(qi, 1, diagonal=True)
        finish()

    @pl.when(qi % 2 == 0)
    def _():
        update(qi, 0, diagonal=True)
        finish()


def _flash(q, k, v, B, S):
    T = q.shape[0]
    tq = _tile(S, 512)
    nq = S // tq
    hq = MLA_NOPE + 2 * MLA_ROPE
    return pl.pallas_call(
        functools.partial(_flash_kernel, tq=tq),
        grid=(B, MLA_HEADS, nq),
        in_specs=[
            pl.BlockSpec((tq, hq), lambda b, h, i: (b * nq + i, h)),
            pl.BlockSpec((S, hq), lambda b, h, i: (b, h)),
            pl.BlockSpec((S, 2 * MLA_V), lambda b, h, i: (b, h)),
        ],
        out_specs=pl.BlockSpec((tq, MLA_V), lambda b, h, i: (b * nq + i, h)),
        out_shape=jax.ShapeDtypeStruct((T, MLA_HEADS * MLA_V), BF16),
        scratch_shapes=[
            pltpu.VMEM((2, tq, tq), F32),
            pltpu.VMEM((2, tq, LANES), F32),
            pltpu.VMEM((tq, LANES), F32),
            pltpu.VMEM((tq, 2 * MLA_V), F32),
        ],
        compiler_params=_cparams(("parallel", "parallel", "arbitrary")),
        name="mla_flash",
    )(q, k, v)


def _mla_rope_table(S):
    half = MLA_ROPE // 2
    pos = jnp.arange(S, dtype=F32)
    inv_freq = ROPE_BASE ** (-jnp.arange(half, dtype=F32) * 2.0 / MLA_ROPE)
    ang = pos[:, None] * inv_freq[None, :]
    cos, sin = jnp.cos(ang), jnp.sin(ang)
    return jnp.concatenate([cos, cos, -sin, sin], axis=-1)


def _swap_halves(w):
    half = w.shape[-1] // 2
    return jnp.concatenate([w[..., half:], w[..., :half]], axis=-1)


def _xattn_kernel(q_ref, k_ref, v_ref, o_ref, s_ref, *, hd):
    scale = hd ** -0.5
    heads = [slice(h * hd, (h + 1) * hd) for h in range(XA_HEADS)]
    for h, cols in enumerate(heads):
        s_ref[h] = _dot_nt(q_ref[:, cols], k_ref[:, cols]) * scale
    for h, cols in enumerate(heads):
        s = s_ref[h]
        e = jnp.exp(s - jnp.max(s, axis=-1, keepdims=True))
        p = e / jnp.sum(e, axis=-1, keepdims=True)
        o_ref[:, cols] = _dot(p.astype(BF16), v_ref[:, cols]).astype(o_ref.dtype)


def _xattn(q, kv, S, M):
    T, D = q.shape
    tm = _tile(S, 512)
    ns = S // tm
    return pl.pallas_call(
        functools.partial(_xattn_kernel, hd=D // XA_HEADS),
        grid=(T // tm,),
        in_specs=[
            pl.BlockSpec((tm, D), lambda i: (i, 0)),
            pl.BlockSpec((M, D), lambda i: (i // ns, 0)),
            pl.BlockSpec((M, D), lambda i: (i // ns, 1)),
        ],
        out_specs=pl.BlockSpec((tm, D), lambda i: (i, 0)),
        out_shape=jax.ShapeDtypeStruct((T, D), BF16),
        scratch_shapes=[pltpu.VMEM((XA_HEADS, tm, M), F32)],
        compiler_params=_cparams(("parallel",)),
        name="xattn",
    )(q, kv, kv)


def kernel(x, mem, ffn1_norm, ffn1_w_gate, ffn1_w_up, ffn1_w_down, mix_norm, w_in, sgu_norm, sgu_w_s, sgu_b, ret_gn, q_norm, w_uq, kv_norm, w_ukv, w_out, xa_norm, mem_norm, xa_wq, xa_wkv, xa_wo, ffn2_norm, ffn2_w_gate, ffn2_w_up, ffn2_w_down, final_norm):
    B, S, D = x.shape
    M = mem.shape[1]
    L = w_in.shape[0]
    T = B * S
    a_width = sgu_norm.shape[-1]
    r_width = ret_gn.shape[-1]
    q_lora = q_norm.shape[-1]
    kv_lora = kv_norm.shape[-1]
    assert a_width == A_GROUPS * CHUNK and r_width == R_HEADS * CHUNK and S % CHUNK == 0
    ret_col0 = 2 * a_width
    mla_col0 = ret_col0 + 4 * r_width
    kr_col0 = mla_col0 + q_lora + kv_lora
    assert w_in.shape[-1] == kr_col0 + MLA_ROPE

    bf = lambda w: w.astype(BF16)
    row = lambda g: g.reshape(g.shape[0], 1, g.shape[1])

    n_in = kr_col0 + MLA_ROPE
    n_in_pad = -(-n_in // COL_CHUNK) * COL_CHUNK
    w_in_x = bf(jnp.pad(w_in, ((0, 0), (0, 0), (0, n_in_pad - n_in))))
    wq4 = w_uq.reshape(L, q_lora, MLA_HEADS, MLA_NOPE + MLA_ROPE)
    wq_x = bf(jnp.concatenate([wq4, _swap_halves(wq4[..., MLA_NOPE:])], axis=-1)).reshape(L, q_lora, -1)
    wkv4 = w_ukv.reshape(L, kv_lora, MLA_HEADS, MLA_NOPE + MLA_V)
    wkv_x = bf(jnp.concatenate([wkv4[..., :MLA_NOPE].reshape(L, kv_lora, -1),
                                wkv4[..., MLA_NOPE:].reshape(L, kv_lora, -1)], axis=-1))
    sgu_bt = jnp.swapaxes(sgu_b, 1, 2)
    ret_tabs = _retention_tables(S)
    mla_rt = _mla_rope_table(S)
    w1g, w1u, w1d = bf(ffn1_w_gate), bf(ffn1_w_up), bf(ffn1_w_down)
    w2g, w2u, w2d = bf(ffn2_w_gate), bf(ffn2_w_up), bf(ffn2_w_down)
    w_out_b, xa_wq_b, xa_wkv_b, xa_wo_b = bf(w_out), bf(xa_wq), bf(xa_wkv), bf(xa_wo)
    fin = final_norm.reshape(1, D)

    h = x.reshape(T, D)
    mem2 = mem.reshape(B * M, D)
    for l in range(L):
        h = _ffn(h, row(ffn1_norm), w1g, w1u, w1d, l, fin, False)
        proj = _norm_mm(h, row(mix_norm), w_in_x, l, F32)
        y_a = _sgu(proj, row(sgu_norm), sgu_w_s, sgu_bt, l, a_width)
        y_r = _retention(proj, *ret_tabs, row(ret_gn), l, B, S, r_width, ret_col0)
        q, k, v = _mla_proj(proj, row(q_norm), row(kv_norm), wq_x, wkv_x, mla_rt, l, S, q_lora, kv_lora, mla_col0)
        y_c = _flash(q, k, v, B, S)
        h = _mm_res([y_a, y_r, y_c], w_out_b, l, h)
        kvm = _norm_mm(mem2, row(mem_norm), xa_wkv_b, l, BF16)
        xq = _norm_mm(h, row(xa_norm), xa_wq_b, l, BF16)
        o = _xattn(xq, kvm, S, M)
        h = _mm_res([o], xa_wo_b, l, h)
        h = _ffn(h, row(ffn2_norm), w2g, w2u, w2d, l, fin, l == L - 1)
    return h.reshape(B, S, D)
```

```python
import functools
import math

import jax
import jax.numpy as jnp
from jax import lax
from jax.experimental import pallas as pl
from jax.experimental.pallas import tpu as pltpu

F32 = jnp.float32
BF16 = jnp.bfloat16

EPS = 1e-6
ROPE_BASE = 10000.0
CHUNK = 128
A_GROUPS = 4
R_HEADS = 4
MLA_HEADS = 8
MLA_NOPE = 128
MLA_ROPE = 64
MLA_V = 128
XA_HEADS = 4
LANES = 128

VMEM_LIMIT = 56 * 1024 * 1024
FFN_VMEM_LIMIT = 62 * 1024 * 1024


def _cparams(sem, vmem_limit=VMEM_LIMIT):
    return pltpu.CompilerParams(dimension_semantics=sem, vmem_limit_bytes=vmem_limit)


def _tile(n, pref):
    t = min(n, pref)
    assert n % t == 0, (n, t)
    return t


def _rms(x, g):
    return x * lax.rsqrt(jnp.mean(x * x, axis=-1, keepdims=True) + EPS) * g


def _silu(x):
    return x * (1.0 / (1.0 + jnp.exp(-x)))


def _dot(a, b):
    return jnp.dot(a, b, preferred_element_type=F32)


def _dot_nt(a, b):
    return lax.dot_general(a, b, (((1,), (1,)), ((), ())), preferred_element_type=F32)


COL_CHUNK = 1024
ROW_CHUNK = 256


def _ffn_kernel(h_ref, g_ref, wg_ref, wu_ref, wd_ref, fg_ref, o_ref, n_ref, *, nf, final):
    f = pl.program_id(1)
    tm, D = o_ref.shape
    row_chunks = [slice(r, r + ROW_CHUNK) for r in range(0, tm, ROW_CHUNK)]

    def step(first, last):
        if first:
            n = jnp.concatenate([_rms(h_ref[rows, :], g_ref[...]).astype(BF16) for rows in row_chunks], axis=0)
            n_ref[...] = n
        else:
            n = n_ref[...]
        a = _dot(n, wg_ref[...])
        b = _dot(n, wu_ref[...])
        hm = (_silu(a) * b).astype(BF16)
        for c in range(0, D, COL_CHUNK):
            cols = slice(c, c + COL_CHUNK)
            acc = _dot(hm, wd_ref[:, cols].astype(BF16))
            if not first:
                acc = o_ref[:, cols] + acc
            o_ref[:, cols] = h_ref[:, cols] + 0.5 * acc if last else acc
        if last and final:
            for rows in row_chunks:
                o_ref[rows, :] = _rms(o_ref[rows, :], fg_ref[...])

    assert nf > 2
    pl.when(f == 0)(functools.partial(step, True, False))
    pl.when((f > 0) & (f < nf - 1))(functools.partial(step, False, False))
    pl.when(f == nf - 1)(functools.partial(step, False, True))


def _ffn(h, g, wg, wu, wd, l, final_g, final):
    T, D = h.shape
    F = wg.shape[-1]
    tm = _tile(T, 1024)
    tf = _tile(F, 512)
    nf = F // tf
    return pl.pallas_call(
        functools.partial(_ffn_kernel, nf=nf, final=final),
        grid=(T // tm, nf),
        in_specs=[
            pl.BlockSpec((tm, D), lambda i, f: (i, 0)),
            pl.BlockSpec((None, 1, D), lambda i, f: (l, 0, 0)),
            pl.BlockSpec((None, D, tf), lambda i, f: (l, 0, f)),
            pl.BlockSpec((None, D, tf), lambda i, f: (l, 0, f)),
            pl.BlockSpec((None, tf, D), lambda i, f: (l, f, 0)),
            pl.BlockSpec((1, D), lambda i, f: (0, 0)),
        ],
        out_specs=pl.BlockSpec((tm, D), lambda i, f: (i, 0)),
        out_shape=jax.ShapeDtypeStruct((T, D), F32),
        scratch_shapes=[pltpu.VMEM((tm, D), BF16)],
        compiler_params=_cparams(("parallel", "arbitrary"), FFN_VMEM_LIMIT),
        name="ffn",
    )(h, g, wg, wu, wd, final_g)


def _resident(block_shape, index_map):
    return pl.BlockSpec(block_shape, index_map, pipeline_mode=pl.Buffered(1))


def _norm_mm_kernel(x_ref, g_ref, w_ref, o_ref):
    n = _rms(x_ref[...], g_ref[...]).astype(BF16)
    N = o_ref.shape[1]
    tn = min(N, COL_CHUNK)
    for c in range(N // tn):
        o_ref[:, c * tn:(c + 1) * tn] = _dot(n, w_ref[:, c * tn:(c + 1) * tn]).astype(o_ref.dtype)


def _norm_mm(x, g, w, l, out_dtype, tm_pref=512):
    M, K = x.shape
    N = w.shape[-1]
    tm = _tile(M, tm_pref)
    return pl.pallas_call(
        _norm_mm_kernel,
        grid=(M // tm,),
        in_specs=[
            pl.BlockSpec((tm, K), lambda i: (i, 0)),
            _resident((None, 1, K), lambda i: (l, 0, 0)),
            _resident((None, K, N), lambda i: (l, 0, 0)),
        ],
        out_specs=pl.BlockSpec((tm, N), lambda i: (i, 0)),
        out_shape=jax.ShapeDtypeStruct((M, N), out_dtype),
        compiler_params=_cparams(("parallel",)),
        name="norm_mm",
    )(x, g, w)


def _mm_res_kernel(*refs, ny):
    ys, ws, h_ref, o_ref = refs[:ny], refs[ny:2 * ny], refs[2 * ny], refs[2 * ny + 1]
    N = o_ref.shape[1]
    tn = min(N, COL_CHUNK)
    for c in range(N // tn):
        cols = slice(c * tn, (c + 1) * tn)
        acc = h_ref[:, cols]
        for y_ref, w_ref in zip(ys, ws):
            acc = acc + _dot(y_ref[...], w_ref[:, cols])
        o_ref[:, cols] = acc


def _mm_res(ys, w, l, h, tm_pref=512):
    T, N = h.shape
    tm = _tile(T, tm_pref)
    y_specs, w_specs, row = [], [], 0
    for y in ys:
        k = y.shape[1]
        assert row % k == 0
        y_specs.append(pl.BlockSpec((tm, k), lambda i: (i, 0)))
        w_specs.append(_resident((None, k, N), functools.partial(lambda i, rb: (l, rb, 0), rb=row // k)))
        row += k
    assert row == w.shape[1]
    return pl.pallas_call(
        functools.partial(_mm_res_kernel, ny=len(ys)),
        grid=(T // tm,),
        in_specs=y_specs + w_specs + [pl.BlockSpec((tm, N), lambda i: (i, 0))],
        out_specs=pl.BlockSpec((tm, N), lambda i: (i, 0)),
        out_shape=jax.ShapeDtypeStruct((T, N), F32),
        compiler_params=_cparams(("parallel",)),
        name="mm_res",
    )(*ys, *([w] * len(ys)), h)


def _sgu_kernel(u_ref, v_ref, g_ref, ws_ref, bt_ref, o_ref, *, nchunk):
    C = CHUNK
    row = lax.broadcasted_iota(jnp.int32, (C, C), 0)
    col = lax.broadcasted_iota(jnp.int32, (C, C), 1)
    wm = [jnp.where(row >= col, ws_ref[g], 0.0).astype(BF16) for g in range(A_GROUPS)]
    bt = bt_ref[...]
    for c in range(nchunk):
        rows = slice(c * C, (c + 1) * C)
        u = jax.nn.gelu(u_ref[rows, :])
        v = _rms(jax.nn.gelu(v_ref[rows, :]), g_ref[...]).astype(BF16)
        for g in range(A_GROUPS):
            cols = slice(g * C, (g + 1) * C)
            z = _dot(wm[g], v[:, cols]) + bt[:, g:g + 1]
            o_ref[rows, cols] = (u[:, cols] * z).astype(o_ref.dtype)


def _sgu(proj, g, ws, bt, l, width):
    T = proj.shape[0]
    ts = _tile(T, 4 * CHUNK)
    return pl.pallas_call(
        functools.partial(_sgu_kernel, nchunk=ts // CHUNK),
        grid=(T // ts,),
        in_specs=[
            pl.BlockSpec((ts, width), lambda i: (i, 0)),
            pl.BlockSpec((ts, width), lambda i: (i, 1)),
            pl.BlockSpec((None, 1, width), lambda i: (l, 0, 0)),
            pl.BlockSpec((None, A_GROUPS, CHUNK, CHUNK), lambda i: (l, 0, 0, 0)),
            pl.BlockSpec((None, CHUNK, A_GROUPS), lambda i: (l, 0, 0)),
        ],
        out_specs=pl.BlockSpec((ts, width), lambda i: (i, 0)),
        out_shape=jax.ShapeDtypeStruct((T, width), BF16),
        compiler_params=_cparams(("parallel",)),
        name="sgu",
    )(proj, proj, g, ws, bt)


def _ret_kernel(q_ref, k_ref, v_ref, gate_ref, cos_ref, sin_ref, dmask_ref, zeta_ref, xi_ref, gch_ref, gn_ref,
                o_ref, state_ref, *, nchunk):
    C = CHUNK
    hd = C
    half = hd // 2

    @pl.when(pl.program_id(1) == 0)
    def _():
        state_ref[...] = jnp.zeros_like(state_ref)

    for c in range(nchunk):
        rows = slice(c * C, (c + 1) * C)
        cos = cos_ref[rows, :]
        sin = sin_ref[rows, :]
        for h in range(R_HEADS):
            cols = slice(h * hd, (h + 1) * hd)
            q = q_ref[rows, cols]
            k = k_ref[rows, cols]
            q = q * cos + pltpu.roll(q, half, 1) * sin
            k = (k * cos + pltpu.roll(k, half, 1) * sin) * (hd ** -0.5)
            vb = v_ref[rows, cols].astype(BF16)
            scores = _dot_nt(q.astype(BF16), k.astype(BF16)) * dmask_ref[h]
            state = state_ref[h]
            y = _dot(scores.astype(BF16), vb) + _dot((q * xi_ref[:, cols]).astype(BF16), state.astype(BF16))
            kz = (k * zeta_ref[:, cols]).astype(BF16)
            state_ref[h] = gch_ref[h] * state + _dot(kz.T, vb)
            mu = jnp.mean(y, axis=-1, keepdims=True)
            yc = y - mu
            var = jnp.mean(yc * yc, axis=-1, keepdims=True)
            yn = yc * lax.rsqrt(var + EPS) * gn_ref[:, cols]
            o_ref[rows, cols] = (_silu(gate_ref[rows, cols]) * yn).astype(o_ref.dtype)


def _retention(proj, cos, sin, dmask, zeta, xi, gch, gn, l, B, S, width, col0):
    T = proj.shape[0]
    tr = _tile(S, 4 * CHUNK)
    ns = S // tr
    cb = col0 // width
    assert col0 % width == 0
    tok = lambda b, n: b * ns + n
    return pl.pallas_call(
        functools.partial(_ret_kernel, nchunk=tr // CHUNK),
        grid=(B, ns),
        in_specs=[
            pl.BlockSpec((tr, width), lambda b, n: (tok(b, n), cb)),
            pl.BlockSpec((tr, width), lambda b, n: (tok(b, n), cb + 1)),
            pl.BlockSpec((tr, width), lambda b, n: (tok(b, n), cb + 2)),
            pl.BlockSpec((tr, width), lambda b, n: (tok(b, n), cb + 3)),
            pl.BlockSpec((tr, CHUNK), lambda b, n: (n, 0)),
            pl.BlockSpec((tr, CHUNK), lambda b, n: (n, 0)),
            pl.BlockSpec((R_HEADS, CHUNK, CHUNK), lambda b, n: (0, 0, 0)),
            pl.BlockSpec((CHUNK, width), lambda b, n: (0, 0)),
            pl.BlockSpec((CHUNK, width), lambda b, n: (0, 0)),
            pl.BlockSpec((R_HEADS, 1, CHUNK), lambda b, n: (0, 0, 0)),
            pl.BlockSpec((None, 1, width), lambda b, n: (l, 0, 0)),
        ],
        out_specs=pl.BlockSpec((tr, width), lambda b, n: (tok(b, n), 0)),
        out_shape=jax.ShapeDtypeStruct((T, width), BF16),
        scratch_shapes=[pltpu.VMEM((R_HEADS, CHUNK, CHUNK), F32)],
        compiler_params=_cparams(("parallel", "arbitrary")),
        name="retention",
    )(proj, proj, proj, proj, cos, sin, dmask, zeta, xi, gch, gn)


def _retention_tables(S):
    C, H, dim = CHUNK, R_HEADS, CHUNK
    half = dim // 2
    pos = jnp.arange(S, dtype=F32)
    inv_freq = ROPE_BASE ** (-jnp.arange(half, dtype=F32) * 2.0 / dim)
    ang = pos[:, None] * inv_freq[None, :]
    cos, sin = jnp.cos(ang), jnp.sin(ang)
    cos_t = jnp.concatenate([cos, cos], axis=-1)
    sin_t = jnp.concatenate([-sin, sin], axis=-1)
    log_g = jnp.log1p(-jnp.exp2(-5.0 - jnp.arange(H, dtype=F32)))
    i = jnp.arange(C, dtype=F32)
    diff = i[:, None] - i[None, :]
    dmask = jnp.where(diff[None] >= 0, jnp.exp(jnp.maximum(diff, 0.0)[None] * log_g[:, None, None]), 0.0)
    zeta = jnp.exp((C - 1 - i)[None, :] * log_g[:, None]).T
    xi = jnp.exp((i + 1)[None, :] * log_g[:, None]).T
    gch = jnp.exp(C * log_g)
    rep = lambda t: jnp.repeat(t, dim, axis=1)
    return cos_t, sin_t, dmask, rep(zeta), rep(xi), jnp.broadcast_to(gch[:, None, None], (H, 1, C))


def _mla_proj_kernel(cq_ref, ckv_ref, kr_ref, qg_ref, kvg_ref, wq_ref, wkv_ref, rt_ref, q_ref, k_ref, v_ref):
    hq = MLA_NOPE + 2 * MLA_ROPE
    scale = (MLA_NOPE + MLA_ROPE) ** -0.5 * math.log2(math.e)
    lane = lax.broadcasted_iota(jnp.int32, (1, 2 * MLA_ROPE), 1)
    keep = jnp.where(lane < MLA_ROPE, scale, 0.0)
    rt = rt_ref[...]

    def rope(y):
        z = y * rt
        return z + pltpu.roll(z, MLA_ROPE, 1)

    x = kr_ref[...]
    half = MLA_ROPE // 2
    swapped = jnp.where(lane < half, pltpu.roll(x, 2 * MLA_ROPE - half, 1), pltpu.roll(x, half, 1))
    kr = jnp.where(lane < MLA_ROPE, x * rt + swapped * pltpu.roll(rt, MLA_ROPE, 1), 0.0).astype(BF16)
    q = _dot(_rms(cq_ref[...], qg_ref[...]).astype(BF16), wq_ref[...])
    kv = _dot(_rms(ckv_ref[...], kvg_ref[...]).astype(BF16), wkv_ref[...])
    for h in range(MLA_HEADS):
        q_ref[:, h * hq:h * hq + MLA_NOPE] = (q[:, h * hq:h * hq + MLA_NOPE] * scale).astype(BF16)
        q_ref[:, h * hq + MLA_NOPE:(h + 1) * hq] = (rope(q[:, h * hq + MLA_NOPE:(h + 1) * hq]) * keep).astype(BF16)
        k_ref[:, h * hq:h * hq + MLA_NOPE] = kv[:, h * MLA_NOPE:(h + 1) * MLA_NOPE].astype(BF16)
        k_ref[:, h * hq + MLA_NOPE:(h + 1) * hq] = kr
        v0 = MLA_HEADS * MLA_NOPE + h * MLA_V
        v_ref[:, 2 * h * MLA_V:(2 * h + 1) * MLA_V] = kv[:, v0:v0 + MLA_V].astype(BF16)
        v_ref[:, (2 * h + 1) * MLA_V:(2 * h + 2) * MLA_V] = jnp.ones((kv.shape[0], MLA_V), BF16)


def _mla_proj(proj, qg, kvg, wq, wkv, rt, l, S, q_lora, kv_lora, col0):
    T = proj.shape[0]
    tm = _tile(S, 512)
    ns = S // tm
    hq = MLA_NOPE + 2 * MLA_ROPE
    c_kv0 = col0 + q_lora
    c_kr0 = c_kv0 + kv_lora
    assert col0 % q_lora == 0 and c_kv0 % kv_lora == 0 and c_kr0 % LANES == 0
    return pl.pallas_call(
        _mla_proj_kernel,
        grid=(T // tm,),
        in_specs=[
            pl.BlockSpec((tm, q_lora), lambda i: (i, col0 // q_lora)),
            pl.BlockSpec((tm, kv_lora), lambda i: (i, c_kv0 // kv_lora)),
            pl.BlockSpec((tm, 2 * MLA_ROPE), lambda i: (i, c_kr0 // LANES)),
            pl.BlockSpec((None, 1, q_lora), lambda i: (l, 0, 0)),
            pl.BlockSpec((None, 1, kv_lora), lambda i: (l, 0, 0)),
            pl.BlockSpec((None, q_lora, MLA_HEADS * hq), lambda i: (l, 0, 0)),
            pl.BlockSpec((None, kv_lora, MLA_HEADS * (MLA_NOPE + MLA_V)), lambda i: (l, 0, 0)),
            pl.BlockSpec((tm, 2 * MLA_ROPE), lambda i: (i % ns, 0)),
        ],
        out_specs=[
            pl.BlockSpec((tm, MLA_HEADS * hq), lambda i: (i, 0)),
            pl.BlockSpec((tm, MLA_HEADS * hq), lambda i: (i, 0)),
            pl.BlockSpec((tm, MLA_HEADS * 2 * MLA_V), lambda i: (i, 0)),
        ],
        out_shape=[
            jax.ShapeDtypeStruct((T, MLA_HEADS * hq), BF16),
            jax.ShapeDtypeStruct((T, MLA_HEADS * hq), BF16),
            jax.ShapeDtypeStruct((T, MLA_HEADS * 2 * MLA_V), BF16),
        ],
        compiler_params=_cparams(("parallel",)),
        name="mla_proj",
    )(proj, proj, proj, qg, kvg, wq, wkv, rt)


def _flash_kernel(q_ref, k_ref, v_ref, o_ref, s_ref, mc_ref, m_ref, acc_ref, *, tq):
    qi = pl.program_id(2)
    rep = tq // LANES
    q = q_ref[...]

    def keys(j):
        return pl.ds(pl.multiple_of(j * tq, tq), tq)

    def rowmax(s):
        return jnp.broadcast_to(jnp.max(s, axis=-1, keepdims=True), (tq, LANES))

    def scores(j, slot):
        s = _dot_nt(q, k_ref[keys(j), :])
        s_ref[slot] = s
        mc_ref[slot] = rowmax(s)

    def update(j, slot, diagonal=False):
        s = s_ref[slot]
        if diagonal:
            row = lax.broadcasted_iota(jnp.int32, s.shape, 0)
            col = lax.broadcasted_iota(jnp.int32, s.shape, 1)
            s = jnp.where(col <= row, s, -1e30)
            m_cur = rowmax(s)
        else:
            m_cur = mc_ref[slot]
        m = m_ref[...]
        m_new = jnp.maximum(m, m_cur)
        m_ref[...] = m_new
        p = jnp.exp2(s - jnp.tile(m_new, (1, rep)))
        alpha = jnp.exp2(m - m_new)
        acc_ref[...] = jnp.tile(alpha, (1, 2)) * acc_ref[...] + _dot(p.astype(BF16), v_ref[keys(j), :])

    def finish():
        o_ref[...] = (acc_ref[:, :MLA_V] / acc_ref[:, MLA_V:]).astype(o_ref.dtype)

    m_ref[...] = jnp.full(m_ref.shape, -1e30, F32)
    acc_ref[...] = jnp.zeros(acc_ref.shape, F32)
    scores(0, 0)

    @pl.loop(0, qi // 2)
    def _(t):
        scores(2 * t + 1, 1)
        update(2 * t, 0)
        scores(2 * t + 2, 0)
        update(2 * t + 1, 1)

    @pl.when(qi % 2 == 1)
    def _():
        scores(qi, 1)
        update(qi - 1, 0)
        update(qi, 1, diagonal=True)
        finish()

    @pl.when(qi % 2 == 0)
    def _():
        update(qi, 0, diagonal=True)
        finish()


def _flash(q, k, v, B, S):
    T = q.shape[0]
    tq = _tile(S, 512)
    nq = S // tq
    hq = MLA_NOPE + 2 * MLA_ROPE
    return pl.pallas_call(
        functools.partial(_flash_kernel, tq=tq),
        grid=(B, MLA_HEADS, nq),
        in_specs=[
            pl.BlockSpec((tq, hq), lambda b, h, i: (b * nq + i, h)),
            pl.BlockSpec((S, hq), lambda b, h, i: (b, h)),
            pl.BlockSpec((S, 2 * MLA_V), lambda b, h, i: (b, h)),
        ],
        out_specs=pl.BlockSpec((tq, MLA_V), lambda b, h, i: (b * nq + i, h)),
        out_shape=jax.ShapeDtypeStruct((T, MLA_HEADS * MLA_V), BF16),
        scratch_shapes=[
            pltpu.VMEM((2, tq, tq), F32),
            pltpu.VMEM((2, tq, LANES), F32),
            pltpu.VMEM((tq, LANES), F32),
            pltpu.VMEM((tq, 2 * MLA_V), F32),
        ],
        compiler_params=_cparams(("parallel", "parallel", "arbitrary")),
        name="mla_flash",
    )(q, k, v)


def _mla_rope_table(S):
    half = MLA_ROPE // 2
    pos = jnp.arange(S, dtype=F32)
    inv_freq = ROPE_BASE ** (-jnp.arange(half, dtype=F32) * 2.0 / MLA_ROPE)
    ang = pos[:, None] * inv_freq[None, :]
    cos, sin = jnp.cos(ang), jnp.sin(ang)
    return jnp.concatenate([cos, cos, -sin, sin], axis=-1)


def _swap_halves(w):
    half = w.shape[-1] // 2
    return jnp.concatenate([w[..., half:], w[..., :half]], axis=-1)


def _xattn_kernel(h_ref, g_ref, wq_ref, k_ref, v_ref, wo_ref, o_ref, q_ref, s_ref, a_ref, *, hd):
    D = o_ref.shape[1]
    scale = hd ** -0.5
    heads = [slice(h * hd, (h + 1) * hd) for h in range(XA_HEADS)]
    col_chunks = [slice(c, c + COL_CHUNK) for c in range(0, D, COL_CHUNK)]
    n = _rms(h_ref[...], g_ref[...]).astype(BF16)
    for cols in col_chunks:
        q_ref[:, cols] = _dot(n, wq_ref[:, cols]).astype(BF16)
    for h, cols in enumerate(heads):
        s_ref[h] = _dot_nt(q_ref[:, cols], k_ref[:, cols]) * scale
    for h, cols in enumerate(heads):
        s = s_ref[h]
        e = jnp.exp(s - jnp.max(s, axis=-1, keepdims=True))
        p = e / jnp.sum(e, axis=-1, keepdims=True)
        a_ref[:, cols] = _dot(p.astype(BF16), v_ref[:, cols]).astype(BF16)
    for cols in col_chunks:
        o_ref[:, cols] = h_ref[:, cols] + _dot(a_ref[...], wo_ref[:, cols])


def _xattn(h, g, wq, kv, wo, l, S, M):
    T, D = h.shape
    tm = _tile(S, 512)
    ns = S // tm
    return pl.pallas_call(
        functools.partial(_xattn_kernel, hd=D // XA_HEADS),
        grid=(T // tm,),
        in_specs=[
            pl.BlockSpec((tm, D), lambda i: (i, 0)),
            _resident((None, 1, D), lambda i: (l, 0, 0)),
            _resident((None, D, D), lambda i: (l, 0, 0)),
            pl.BlockSpec((M, D), lambda i: (i // ns, 0)),
            pl.BlockSpec((M, D), lambda i: (i // ns, 1)),
            _resident((None, D, D), lambda i: (l, 0, 0)),
        ],
        out_specs=pl.BlockSpec((tm, D), lambda i: (i, 0)),
        out_shape=jax.ShapeDtypeStruct((T, D), F32),
        scratch_shapes=[pltpu.VMEM((tm, D), BF16), pltpu.VMEM((XA_HEADS, tm, M), F32), pltpu.VMEM((tm, D), BF16)],
        compiler_params=_cparams(("parallel",)),
        name="xattn",
    )(h, g, wq, kv, kv, wo)


def kernel(x, mem, ffn1_norm, ffn1_w_gate, ffn1_w_up, ffn1_w_down, mix_norm, w_in, sgu_norm, sgu_w_s, sgu_b, ret_gn, q_norm, w_uq, kv_norm, w_ukv, w_out, xa_norm, mem_norm, xa_wq, xa_wkv, xa_wo, ffn2_norm, ffn2_w_gate, ffn2_w_up, ffn2_w_down, final_norm):
    B, S, D = x.shape
    M = mem.shape[1]
    L = w_in.shape[0]
    T = B * S
    a_width = sgu_norm.shape[-1]
    r_width = ret_gn.shape[-1]
    q_lora = q_norm.shape[-1]
    kv_lora = kv_norm.shape[-1]
    assert a_width == A_GROUPS * CHUNK and r_width == R_HEADS * CHUNK and S % CHUNK == 0
    ret_col0 = 2 * a_width
    mla_col0 = ret_col0 + 4 * r_width
    kr_col0 = mla_col0 + q_lora + kv_lora
    assert w_in.shape[-1] == kr_col0 + MLA_ROPE

    bf = lambda w: w.astype(BF16)
    row = lambda g: g.reshape(g.shape[0], 1, g.shape[1])

    n_in = kr_col0 + MLA_ROPE
    n_in_pad = -(-n_in // COL_CHUNK) * COL_CHUNK
    w_in_x = bf(jnp.pad(w_in, ((0, 0), (0, 0), (0, n_in_pad - n_in))))
    wq4 = w_uq.reshape(L, q_lora, MLA_HEADS, MLA_NOPE + MLA_ROPE)
    wq_x = bf(jnp.concatenate([wq4, _swap_halves(wq4[..., MLA_NOPE:])], axis=-1)).reshape(L, q_lora, -1)
    wkv4 = w_ukv.reshape(L, kv_lora, MLA_HEADS, MLA_NOPE + MLA_V)
    wkv_x = bf(jnp.concatenate([wkv4[..., :MLA_NOPE].reshape(L, kv_lora, -1),
                                wkv4[..., MLA_NOPE:].reshape(L, kv_lora, -1)], axis=-1))
    sgu_bt = jnp.swapaxes(sgu_b, 1, 2)
    ret_tabs = _retention_tables(S)
    mla_rt = _mla_rope_table(S)
    w1g, w1u, w1d = bf(ffn1_w_gate), bf(ffn1_w_up), ffn1_w_down
    w2g, w2u, w2d = bf(ffn2_w_gate), bf(ffn2_w_up), ffn2_w_down
    w_out_b, xa_wq_b, xa_wkv_b, xa_wo_b = bf(w_out), bf(xa_wq), bf(xa_wkv), bf(xa_wo)
    fin = final_norm.reshape(1, D)

    h = x.reshape(T, D)
    mem2 = mem.reshape(B * M, D)
    for l in range(L):
        h = _ffn(h, row(ffn1_norm), w1g, w1u, w1d, l, fin, False)
        proj = _norm_mm(h, row(mix_norm), w_in_x, l, F32)
        y_a = _sgu(proj, row(sgu_norm), sgu_w_s, sgu_bt, l, a_width)
        y_r = _retention(proj, *ret_tabs, row(ret_gn), l, B, S, r_width, ret_col0)
        q, k, v = _mla_proj(proj, row(q_norm), row(kv_norm), wq_x, wkv_x, mla_rt, l, S, q_lora, kv_lora, mla_col0)
        y_c = _flash(q, k, v, B, S)
        h = _mm_res([y_a, y_r, y_c], w_out_b, l, h)
        kvm = _norm_mm(mem2, row(mem_norm), xa_wkv_b, l, BF16)
        h = _xattn(h, row(xa_norm), xa_wq_b, kvm, xa_wo_b, l, S, M)
        h = _ffn(h, row(ffn2_norm), w2g, w2u, w2d, l, fin, l == L - 1)
    return h.reshape(B, S, D)
```

```python
import functools
import math

import jax
import jax.numpy as jnp
from jax import lax
from jax.experimental import pallas as pl
from jax.experimental.pallas import tpu as pltpu

F32 = jnp.float32
BF16 = jnp.bfloat16

EPS = 1e-6
ROPE_BASE = 10000.0
CHUNK = 128
A_GROUPS = 4
R_HEADS = 4
MLA_HEADS = 8
MLA_NOPE = 128
MLA_ROPE = 64
MLA_V = 128
XA_HEADS = 4
LANES = 128

VMEM_LIMIT = 56 * 1024 * 1024
FFN_VMEM_LIMIT = 62 * 1024 * 1024


def _cparams(sem, vmem_limit=VMEM_LIMIT):
    return pltpu.CompilerParams(dimension_semantics=sem, vmem_limit_bytes=vmem_limit)


def _tile(n, pref):
    t = min(n, pref)
    assert n % t == 0, (n, t)
    return t


def _rms(x, g):
    return x * lax.rsqrt(jnp.mean(x * x, axis=-1, keepdims=True) + EPS) * g


def _silu(x):
    return x * (1.0 / (1.0 + jnp.exp(-x)))


def _dot(a, b):
    return jnp.dot(a, b, preferred_element_type=F32)


def _dot_nt(a, b):
    return lax.dot_general(a, b, (((1,), (1,)), ((), ())), preferred_element_type=F32)


COL_CHUNK = 1024
ROW_CHUNK = 256


def _ffn_kernel(h_ref, g_ref, wg_ref, wu_ref, wd_ref, fg_ref, o_ref, n_ref, *, nf, final):
    f = pl.program_id(1)
    tm, D = o_ref.shape
    row_chunks = [slice(r, r + ROW_CHUNK) for r in range(0, tm, ROW_CHUNK)]

    def step(first, last):
        if first:
            n = jnp.concatenate([_rms(h_ref[rows, :], g_ref[...]).astype(BF16) for rows in row_chunks], axis=0)
            n_ref[...] = n
        else:
            n = n_ref[...]
        a = _dot(n, wg_ref[...])
        b = _dot(n, wu_ref[...].astype(BF16))
        hm = (_silu(a) * b).astype(BF16)
        for c in range(0, D, COL_CHUNK):
            cols = slice(c, c + COL_CHUNK)
            acc = _dot(hm, wd_ref[:, cols].astype(BF16))
            if not first:
                acc = o_ref[:, cols] + acc
            o_ref[:, cols] = h_ref[:, cols] + 0.5 * acc if last else acc
        if last and final:
            for rows in row_chunks:
                o_ref[rows, :] = _rms(o_ref[rows, :], fg_ref[...])

    assert nf > 2
    pl.when(f == 0)(functools.partial(step, True, False))
    pl.when((f > 0) & (f < nf - 1))(functools.partial(step, False, False))
    pl.when(f == nf - 1)(functools.partial(step, False, True))


def _ffn(h, g, wg, wu, wd, l, final_g, final):
    T, D = h.shape
    F = wg.shape[-1]
    tm = _tile(T, 1024)
    tf = _tile(F, 512)
    nf = F // tf
    return pl.pallas_call(
        functools.partial(_ffn_kernel, nf=nf, final=final),
        grid=(T // tm, nf),
        in_specs=[
            pl.BlockSpec((tm, D), lambda i, f: (i, 0)),
            pl.BlockSpec((None, 1, D), lambda i, f: (l, 0, 0)),
            pl.BlockSpec((None, D, tf), lambda i, f: (l, 0, f)),
            pl.BlockSpec((None, D, tf), lambda i, f: (l, 0, f)),
            pl.BlockSpec((None, tf, D), lambda i, f: (l, f, 0)),
            pl.BlockSpec((1, D), lambda i, f: (0, 0)),
        ],
        out_specs=pl.BlockSpec((tm, D), lambda i, f: (i, 0)),
        out_shape=jax.ShapeDtypeStruct((T, D), F32),
        scratch_shapes=[pltpu.VMEM((tm, D), BF16)],
        compiler_params=_cparams(("parallel", "arbitrary"), FFN_VMEM_LIMIT),
        name="ffn",
    )(h, g, wg, wu, wd, final_g)


def _resident(block_shape, index_map):
    return pl.BlockSpec(block_shape, index_map, pipeline_mode=pl.Buffered(1))


def _norm_mm_kernel(x_ref, g_ref, w_ref, o_ref):
    n = _rms(x_ref[...], g_ref[...]).astype(BF16)
    N = o_ref.shape[1]
    tn = min(N, COL_CHUNK)
    for c in range(N // tn):
        o_ref[:, c * tn:(c + 1) * tn] = _dot(n, w_ref[:, c * tn:(c + 1) * tn]).astype(o_ref.dtype)


def _norm_mm(x, g, w, l, out_dtype, tm_pref=512):
    M, K = x.shape
    N = w.shape[-1]
    tm = _tile(M, tm_pref)
    return pl.pallas_call(
        _norm_mm_kernel,
        grid=(M // tm,),
        in_specs=[
            pl.BlockSpec((tm, K), lambda i: (i, 0)),
            _resident((None, 1, K), lambda i: (l, 0, 0)),
            _resident((None, K, N), lambda i: (l, 0, 0)),
        ],
        out_specs=pl.BlockSpec((tm, N), lambda i: (i, 0)),
        out_shape=jax.ShapeDtypeStruct((M, N), out_dtype),
        compiler_params=_cparams(("parallel",)),
        name="norm_mm",
    )(x, g, w)


def _mm_res_kernel(*refs, ny):
    ys, ws, h_ref, o_ref = refs[:ny], refs[ny:2 * ny], refs[2 * ny], refs[2 * ny + 1]
    N = o_ref.shape[1]
    tn = min(N, COL_CHUNK)
    for c in range(N // tn):
        cols = slice(c * tn, (c + 1) * tn)
        acc = h_ref[:, cols]
        for y_ref, w_ref in zip(ys, ws):
            acc = acc + _dot(y_ref[...], w_ref[:, cols])
        o_ref[:, cols] = acc


def _mm_res(ys, w, l, h, tm_pref=512):
    T, N = h.shape
    tm = _tile(T, tm_pref)
    y_specs, w_specs, row = [], [], 0
    for y in ys:
        k = y.shape[1]
        assert row % k == 0
        y_specs.append(pl.BlockSpec((tm, k), lambda i: (i, 0)))
        w_specs.append(_resident((None, k, N), functools.partial(lambda i, rb: (l, rb, 0), rb=row // k)))
        row += k
    assert row == w.shape[1]
    return pl.pallas_call(
        functools.partial(_mm_res_kernel, ny=len(ys)),
        grid=(T // tm,),
        in_specs=y_specs + w_specs + [pl.BlockSpec((tm, N), lambda i: (i, 0))],
        out_specs=pl.BlockSpec((tm, N), lambda i: (i, 0)),
        out_shape=jax.ShapeDtypeStruct((T, N), F32),
        compiler_params=_cparams(("parallel",)),
        name="mm_res",
    )(*ys, *([w] * len(ys)), h)


def _sgu_kernel(u_ref, v_ref, g_ref, ws_ref, bt_ref, o_ref, *, nchunk):
    C = CHUNK
    row = lax.broadcasted_iota(jnp.int32, (C, C), 0)
    col = lax.broadcasted_iota(jnp.int32, (C, C), 1)
    wm = [jnp.where(row >= col, ws_ref[g], 0.0).astype(BF16) for g in range(A_GROUPS)]
    bt = bt_ref[...]
    for c in range(nchunk):
        rows = slice(c * C, (c + 1) * C)
        u = jax.nn.gelu(u_ref[rows, :])
        v = _rms(jax.nn.gelu(v_ref[rows, :]), g_ref[...]).astype(BF16)
        for g in range(A_GROUPS):
            cols = slice(g * C, (g + 1) * C)
            z = _dot(wm[g], v[:, cols]) + bt[:, g:g + 1]
            o_ref[rows, cols] = (u[:, cols] * z).astype(o_ref.dtype)


def _sgu(proj, g, ws, bt, l, width):
    T = proj.shape[0]
    ts = _tile(T, 4 * CHUNK)
    return pl.pallas_call(
        functools.partial(_sgu_kernel, nchunk=ts // CHUNK),
        grid=(T // ts,),
        in_specs=[
            pl.BlockSpec((ts, width), lambda i: (i, 0)),
            pl.BlockSpec((ts, width), lambda i: (i, 1)),
            pl.BlockSpec((None, 1, width), lambda i: (l, 0, 0)),
            pl.BlockSpec((None, A_GROUPS, CHUNK, CHUNK), lambda i: (l, 0, 0, 0)),
            pl.BlockSpec((None, CHUNK, A_GROUPS), lambda i: (l, 0, 0)),
        ],
        out_specs=pl.BlockSpec((ts, width), lambda i: (i, 0)),
        out_shape=jax.ShapeDtypeStruct((T, width), BF16),
        compiler_params=_cparams(("parallel",)),
        name="sgu",
    )(proj, proj, g, ws, bt)


def _ret_kernel(q_ref, k_ref, v_ref, gate_ref, cos_ref, sin_ref, dmask_ref, zeta_ref, xi_ref, gch_ref, gn_ref,
                o_ref, state_ref, *, nchunk):
    C = CHUNK
    hd = C
    half = hd // 2

    @pl.when(pl.program_id(1) == 0)
    def _():
        state_ref[...] = jnp.zeros_like(state_ref)

    for c in range(nchunk):
        rows = slice(c * C, (c + 1) * C)
        cos = cos_ref[rows, :]
        sin = sin_ref[rows, :]
        for h in range(R_HEADS):
            cols = slice(h * hd, (h + 1) * hd)
            q = q_ref[rows, cols]
            k = k_ref[rows, cols]
            q = q * cos + pltpu.roll(q, half, 1) * sin
            k = (k * cos + pltpu.roll(k, half, 1) * sin) * (hd ** -0.5)
            vb = v_ref[rows, cols].astype(BF16)
            scores = _dot_nt(q.astype(BF16), k.astype(BF16)) * dmask_ref[h]
            state = state_ref[h]
            y = _dot(scores.astype(BF16), vb) + _dot((q * xi_ref[:, cols]).astype(BF16), state.astype(BF16))
            kz = (k * zeta_ref[:, cols]).astype(BF16)
            state_ref[h] = gch_ref[h] * state + _dot(kz.T, vb)
            mu = jnp.mean(y, axis=-1, keepdims=True)
            yc = y - mu
            var = jnp.mean(yc * yc, axis=-1, keepdims=True)
            yn = yc * lax.rsqrt(var + EPS) * gn_ref[:, cols]
            o_ref[rows, cols] = (_silu(gate_ref[rows, cols]) * yn).astype(o_ref.dtype)


def _retention(proj, cos, sin, dmask, zeta, xi, gch, gn, l, B, S, width, col0):
    T = proj.shape[0]
    tr = _tile(S, 4 * CHUNK)
    ns = S // tr
    cb = col0 // width
    assert col0 % width == 0
    tok = lambda b, n: b * ns + n
    return pl.pallas_call(
        functools.partial(_ret_kernel, nchunk=tr // CHUNK),
        grid=(B, ns),
        in_specs=[
            pl.BlockSpec((tr, width), lambda b, n: (tok(b, n), cb)),
            pl.BlockSpec((tr, width), lambda b, n: (tok(b, n), cb + 1)),
            pl.BlockSpec((tr, width), lambda b, n: (tok(b, n), cb + 2)),
            pl.BlockSpec((tr, width), lambda b, n: (tok(b, n), cb + 3)),
            pl.BlockSpec((tr, CHUNK), lambda b, n: (n, 0)),
            pl.BlockSpec((tr, CHUNK), lambda b, n: (n, 0)),
            pl.BlockSpec((R_HEADS, CHUNK, CHUNK), lambda b, n: (0, 0, 0)),
            pl.BlockSpec((CHUNK, width), lambda b, n: (0, 0)),
            pl.BlockSpec((CHUNK, width), lambda b, n: (0, 0)),
            pl.BlockSpec((R_HEADS, 1, CHUNK), lambda b, n: (0, 0, 0)),
            pl.BlockSpec((None, 1, width), lambda b, n: (l, 0, 0)),
        ],
        out_specs=pl.BlockSpec((tr, width), lambda b, n: (tok(b, n), 0)),
        out_shape=jax.ShapeDtypeStruct((T, width), BF16),
        scratch_shapes=[pltpu.VMEM((R_HEADS, CHUNK, CHUNK), F32)],
        compiler_params=_cparams(("parallel", "arbitrary")),
        name="retention",
    )(proj, proj, proj, proj, cos, sin, dmask, zeta, xi, gch, gn)


def _retention_tables(S):
    C, H, dim = CHUNK, R_HEADS, CHUNK
    half = dim // 2
    pos = jnp.arange(S, dtype=F32)
    inv_freq = ROPE_BASE ** (-jnp.arange(half, dtype=F32) * 2.0 / dim)
    ang = pos[:, None] * inv_freq[None, :]
    cos, sin = jnp.cos(ang), jnp.sin(ang)
    cos_t = jnp.concatenate([cos, cos], axis=-1)
    sin_t = jnp.concatenate([-sin, sin], axis=-1)
    log_g = jnp.log1p(-jnp.exp2(-5.0 - jnp.arange(H, dtype=F32)))
    i = jnp.arange(C, dtype=F32)
    diff = i[:, None] - i[None, :]
    dmask = jnp.where(diff[None] >= 0, jnp.exp(jnp.maximum(diff, 0.0)[None] * log_g[:, None, None]), 0.0)
    zeta = jnp.exp((C - 1 - i)[None, :] * log_g[:, None]).T
    xi = jnp.exp((i + 1)[None, :] * log_g[:, None]).T
    gch = jnp.exp(C * log_g)
    rep = lambda t: jnp.repeat(t, dim, axis=1)
    return cos_t, sin_t, dmask, rep(zeta), rep(xi), jnp.broadcast_to(gch[:, None, None], (H, 1, C))


def _mla_proj_kernel(cq_ref, ckv_ref, kr_ref, qg_ref, kvg_ref, wq_ref, wkv_ref, rt_ref, q_ref, k_ref, v_ref):
    hq = MLA_NOPE + 2 * MLA_ROPE
    scale = (MLA_NOPE + MLA_ROPE) ** -0.5 * math.log2(math.e)
    lane = lax.broadcasted_iota(jnp.int32, (1, 2 * MLA_ROPE), 1)
    keep = jnp.where(lane < MLA_ROPE, scale, 0.0)
    rt = rt_ref[...]

    def rope(y):
        z = y * rt
        return z + pltpu.roll(z, MLA_ROPE, 1)

    x = kr_ref[...]
    half = MLA_ROPE // 2
    swapped = jnp.where(lane < half, pltpu.roll(x, 2 * MLA_ROPE - half, 1), pltpu.roll(x, half, 1))
    kr = jnp.where(lane < MLA_ROPE, x * rt + swapped * pltpu.roll(rt, MLA_ROPE, 1), 0.0).astype(BF16)
    q = _dot(_rms(cq_ref[...], qg_ref[...]).astype(BF16), wq_ref[...])
    kv = _dot(_rms(ckv_ref[...], kvg_ref[...]).astype(BF16), wkv_ref[...])
    for h in range(MLA_HEADS):
        q_ref[:, h * hq:h * hq + MLA_NOPE] = (q[:, h * hq:h * hq + MLA_NOPE] * scale).astype(BF16)
        q_ref[:, h * hq + MLA_NOPE:(h + 1) * hq] = (rope(q[:, h * hq + MLA_NOPE:(h + 1) * hq]) * keep).astype(BF16)
        k_ref[:, h * hq:h * hq + MLA_NOPE] = kv[:, h * MLA_NOPE:(h + 1) * MLA_NOPE].astype(BF16)
        k_ref[:, h * hq + MLA_NOPE:(h + 1) * hq] = kr
        v0 = MLA_HEADS * MLA_NOPE + h * MLA_V
        v_ref[:, 2 * h * MLA_V:(2 * h + 1) * MLA_V] = kv[:, v0:v0 + MLA_V].astype(BF16)
        v_ref[:, (2 * h + 1) * MLA_V:(2 * h + 2) * MLA_V] = jnp.ones((kv.shape[0], MLA_V), BF16)


def _mla_proj(proj, qg, kvg, wq, wkv, rt, l, S, q_lora, kv_lora, col0):
    T = proj.shape[0]
    tm = _tile(S, 512)
    ns = S // tm
    hq = MLA_NOPE + 2 * MLA_ROPE
    c_kv0 = col0 + q_lora
    c_kr0 = c_kv0 + kv_lora
    assert col0 % q_lora == 0 and c_kv0 % kv_lora == 0 and c_kr0 % LANES == 0
    return pl.pallas_call(
        _mla_proj_kernel,
        grid=(T // tm,),
        in_specs=[
            pl.BlockSpec((tm, q_lora), lambda i: (i, col0 // q_lora)),
            pl.BlockSpec((tm, kv_lora), lambda i: (i, c_kv0 // kv_lora)),
            pl.BlockSpec((tm, 2 * MLA_ROPE), lambda i: (i, c_kr0 // LANES)),
            pl.BlockSpec((None, 1, q_lora), lambda i: (l, 0, 0)),
            pl.BlockSpec((None, 1, kv_lora), lambda i: (l, 0, 0)),
            pl.BlockSpec((None, q_lora, MLA_HEADS * hq), lambda i: (l, 0, 0)),
            pl.BlockSpec((None, kv_lora, MLA_HEADS * (MLA_NOPE + MLA_V)), lambda i: (l, 0, 0)),
            pl.BlockSpec((tm, 2 * MLA_ROPE), lambda i: (i % ns, 0)),
        ],
        out_specs=[
            pl.BlockSpec((tm, MLA_HEADS * hq), lambda i: (i, 0)),
            pl.BlockSpec((tm, MLA_HEADS * hq), lambda i: (i, 0)),
            pl.BlockSpec((tm, MLA_HEADS * 2 * MLA_V), lambda i: (i, 0)),
        ],
        out_shape=[
            jax.ShapeDtypeStruct((T, MLA_HEADS * hq), BF16),
            jax.ShapeDtypeStruct((T, MLA_HEADS * hq), BF16),
            jax.ShapeDtypeStruct((T, MLA_HEADS * 2 * MLA_V), BF16),
        ],
        compiler_params=_cparams(("parallel",)),
        name="mla_proj",
    )(proj, proj, proj, qg, kvg, wq, wkv, rt)


def _flash_kernel(q_ref, k_ref, v_ref, o_ref, s_ref, mc_ref, m_ref, acc_ref, *, tq):
    qi = pl.program_id(2)
    nq = k_ref.shape[0] // tq
    rep = tq // LANES
    q = q_ref[...]

    def keys(j):
        return slice(j * tq, (j + 1) * tq)

    def rowmax(s):
        return jnp.broadcast_to(jnp.max(s, axis=-1, keepdims=True), (tq, LANES))

    def scores(j, slot):
        s = _dot_nt(q, k_ref[keys(j), :])
        s_ref[slot] = s
        mc_ref[slot] = rowmax(s)

    def update(j, slot, diagonal=False):
        s = s_ref[slot]
        if diagonal:
            row = lax.broadcasted_iota(jnp.int32, s.shape, 0)
            col = lax.broadcasted_iota(jnp.int32, s.shape, 1)
            s = jnp.where(col <= row, s, -1e30)
            m_cur = rowmax(s)
        else:
            m_cur = mc_ref[slot]
        m = m_ref[...]
        m_new = jnp.maximum(m, m_cur)
        m_ref[...] = m_new
        p = jnp.exp2(s - jnp.tile(m_new, (1, rep)))
        alpha = jnp.exp2(m - m_new)
        acc_ref[...] = jnp.tile(alpha, (1, 2)) * acc_ref[...] + _dot(p.astype(BF16), v_ref[keys(j), :])

    def finish():
        o_ref[...] = (acc_ref[:, :MLA_V] / acc_ref[:, MLA_V:]).astype(o_ref.dtype)

    def tile(c):
        m_ref[...] = jnp.full(m_ref.shape, -1e30, F32)
        acc_ref[...] = jnp.zeros(acc_ref.shape, F32)
        scores(0, 0)
        for j in range(c):
            scores(j + 1, (j + 1) % 2)
            update(j, j % 2)
        update(c, c % 2, diagonal=True)
        finish()

    for c in range(nq):
        pl.when(qi == c)(functools.partial(tile, c))


def _flash(q, k, v, B, S):
    T = q.shape[0]
    tq = _tile(S, 512)
    nq = S // tq
    hq = MLA_NOPE + 2 * MLA_ROPE
    return pl.pallas_call(
        functools.partial(_flash_kernel, tq=tq),
        grid=(B, MLA_HEADS, nq),
        in_specs=[
            pl.BlockSpec((tq, hq), lambda b, h, i: (b * nq + i, h)),
            pl.BlockSpec((S, hq), lambda b, h, i: (b, h)),
            pl.BlockSpec((S, 2 * MLA_V), lambda b, h, i: (b, h)),
        ],
        out_specs=pl.BlockSpec((tq, MLA_V), lambda b, h, i: (b * nq + i, h)),
        out_shape=jax.ShapeDtypeStruct((T, MLA_HEADS * MLA_V), BF16),
        scratch_shapes=[
            pltpu.VMEM((2, tq, tq), F32),
            pltpu.VMEM((2, tq, LANES), F32),
            pltpu.VMEM((tq, LANES), F32),
            pltpu.VMEM((tq, 2 * MLA_V), F32),
        ],
        compiler_params=_cparams(("parallel", "parallel", "arbitrary")),
        name="mla_flash",
    )(q, k, v)


def _mla_rope_table(S):
    half = MLA_ROPE // 2
    pos = jnp.arange(S, dtype=F32)
    inv_freq = ROPE_BASE ** (-jnp.arange(half, dtype=F32) * 2.0 / MLA_ROPE)
    ang = pos[:, None] * inv_freq[None, :]
    cos, sin = jnp.cos(ang), jnp.sin(ang)
    return jnp.concatenate([cos, cos, -sin, sin], axis=-1)


def _swap_halves(w):
    half = w.shape[-1] // 2
    return jnp.concatenate([w[..., half:], w[..., :half]], axis=-1)


def _xattn_kernel(h_ref, g_ref, wq_ref, k_ref, v_ref, wo_ref, o_ref, q_ref, s_ref, a_ref, *, hd):
    D = o_ref.shape[1]
    scale = hd ** -0.5
    heads = [slice(h * hd, (h + 1) * hd) for h in range(XA_HEADS)]
    col_chunks = [slice(c, c + COL_CHUNK) for c in range(0, D, COL_CHUNK)]
    n = _rms(h_ref[...], g_ref[...]).astype(BF16)
    for cols in col_chunks:
        q_ref[:, cols] = _dot(n, wq_ref[:, cols]).astype(BF16)
    for h, cols in enumerate(heads):
        s_ref[h] = _dot_nt(q_ref[:, cols], k_ref[:, cols]) * scale
    for h, cols in enumerate(heads):
        s = s_ref[h]
        e = jnp.exp(s - jnp.max(s, axis=-1, keepdims=True))
        p = e / jnp.sum(e, axis=-1, keepdims=True)
        a_ref[:, cols] = _dot(p.astype(BF16), v_ref[:, cols]).astype(BF16)
    for cols in col_chunks:
        o_ref[:, cols] = h_ref[:, cols] + _dot(a_ref[...], wo_ref[:, cols])


def _xattn(h, g, wq, kv, wo, l, S, M):
    T, D = h.shape
    tm = _tile(S, 512)
    ns = S // tm
    return pl.pallas_call(
        functools.partial(_xattn_kernel, hd=D // XA_HEADS),
        grid=(T // tm,),
        in_specs=[
            pl.BlockSpec((tm, D), lambda i: (i, 0)),
            _resident((None, 1, D), lambda i: (l, 0, 0)),
            _resident((None, D, D), lambda i: (l, 0, 0)),
            pl.BlockSpec((M, D), lambda i: (i // ns, 0)),
            pl.BlockSpec((M, D), lambda i: (i // ns, 1)),
            _resident((None, D, D), lambda i: (l, 0, 0)),
        ],
        out_specs=pl.BlockSpec((tm, D), lambda i: (i, 0)),
        out_shape=jax.ShapeDtypeStruct((T, D), F32),
        scratch_shapes=[pltpu.VMEM((tm, D), BF16), pltpu.VMEM((XA_HEADS, tm, M), F32), pltpu.VMEM((tm, D), BF16)],
        compiler_params=_cparams(("parallel",)),
        name="xattn",
    )(h, g, wq, kv, kv, wo)


def kernel(x, mem, ffn1_norm, ffn1_w_gate, ffn1_w_up, ffn1_w_down, mix_norm, w_in, sgu_norm, sgu_w_s, sgu_b, ret_gn, q_norm, w_uq, kv_norm, w_ukv, w_out, xa_norm, mem_norm, xa_wq, xa_wkv, xa_wo, ffn2_norm, ffn2_w_gate, ffn2_w_up, ffn2_w_down, final_norm):
    B, S, D = x.shape
    M = mem.shape[1]
    L = w_in.shape[0]
    T = B * S
    a_width = sgu_norm.shape[-1]
    r_width = ret_gn.shape[-1]
    q_lora = q_norm.shape[-1]
    kv_lora = kv_norm.shape[-1]
    assert a_width == A_GROUPS * CHUNK and r_width == R_HEADS * CHUNK and S % CHUNK == 0
    ret_col0 = 2 * a_width
    mla_col0 = ret_col0 + 4 * r_width
    kr_col0 = mla_col0 + q_lora + kv_lora
    assert w_in.shape[-1] == kr_col0 + MLA_ROPE

    bf = lambda w: w.astype(BF16)
    row = lambda g: g.reshape(g.shape[0], 1, g.shape[1])

    n_in = kr_col0 + MLA_ROPE
    n_in_pad = -(-n_in // COL_CHUNK) * COL_CHUNK
    w_in_x = bf(jnp.pad(w_in, ((0, 0), (0, 0), (0, n_in_pad - n_in))))
    wq4 = w_uq.reshape(L, q_lora, MLA_HEADS, MLA_NOPE + MLA_ROPE)
    wq_x = bf(jnp.concatenate([wq4, _swap_halves(wq4[..., MLA_NOPE:])], axis=-1)).reshape(L, q_lora, -1)
    wkv4 = w_ukv.reshape(L, kv_lora, MLA_HEADS, MLA_NOPE + MLA_V)
    wkv_x = bf(jnp.concatenate([wkv4[..., :MLA_NOPE].reshape(L, kv_lora, -1),
                                wkv4[..., MLA_NOPE:].reshape(L, kv_lora, -1)], axis=-1))
    sgu_bt = jnp.swapaxes(sgu_b, 1, 2)
    ret_tabs = _retention_tables(S)
    mla_rt = _mla_rope_table(S)
    w1g, w1u, w1d = bf(ffn1_w_gate), ffn1_w_up, ffn1_w_down
    w2g, w2u, w2d = bf(ffn2_w_gate), ffn2_w_up, ffn2_w_down
    w_out_b, xa_wq_b, xa_wkv_b, xa_wo_b = bf(w_out), bf(xa_wq), bf(xa_wkv), bf(xa_wo)
    fin = final_norm.reshape(1, D)

    h = x.reshape(T, D)
    mem2 = mem.reshape(B * M, D)
    for l in range(L):
        h = _ffn(h, row(ffn1_norm), w1g, w1u, w1d, l, fin, False)
        proj = _norm_mm(h, row(mix_norm), w_in_x, l, F32)
        y_a = _sgu(proj, row(sgu_norm), sgu_w_s, sgu_bt, l, a_width)
        y_r = _retention(proj, *ret_tabs, row(ret_gn), l, B, S, r_width, ret_col0)
        q, k, v = _mla_proj(proj, row(q_norm), row(kv_norm), wq_x, wkv_x, mla_rt, l, S, q_lora, kv_lora, mla_col0)
        y_c = _flash(q, k, v, B, S)
        h = _mm_res([y_a, y_r, y_c], w_out_b, l, h)
        kvm = _norm_mm(mem2, row(mem_norm), xa_wkv_b, l, BF16)
        h = _xattn(h, row(xa_norm), xa_wq_b, kvm, xa_wo_b, l, S, M)
        h = _ffn(h, row(ffn2_norm), w2g, w2u, w2d, l, fin, l == L - 1)
    return h.reshape(B, S, D)
```

```python
import functools
import math

import jax
import jax.numpy as jnp
from jax import lax
from jax.experimental import pallas as pl
from jax.experimental.pallas import tpu as pltpu

F32 = jnp.float32
BF16 = jnp.bfloat16

EPS = 1e-6
ROPE_BASE = 10000.0
CHUNK = 128
A_GROUPS = 4
R_HEADS = 4
MLA_HEADS = 8
MLA_NOPE = 128
MLA_ROPE = 64
MLA_V = 128
XA_HEADS = 4
LANES = 128

VMEM_LIMIT = 56 * 1024 * 1024
FFN_VMEM_LIMIT = 62 * 1024 * 1024


def _cparams(sem, vmem_limit=VMEM_LIMIT):
    return pltpu.CompilerParams(dimension_semantics=sem, vmem_limit_bytes=vmem_limit)


def _tile(n, pref):
    t = min(n, pref)
    assert n % t == 0, (n, t)
    return t


def _rms(x, g):
    return x * lax.rsqrt(jnp.mean(x * x, axis=-1, keepdims=True) + EPS) * g


def _silu(x):
    return x * (1.0 / (1.0 + jnp.exp(-x)))


def _dot(a, b):
    return jnp.dot(a, b, preferred_element_type=F32)


def _dot_nt(a, b):
    return lax.dot_general(a, b, (((1,), (1,)), ((), ())), preferred_element_type=F32)


COL_CHUNK = 1024
ROW_CHUNK = 256


def _ffn_kernel(h_ref, g_ref, wg_ref, wu_ref, wd_ref, fg_ref, o_ref, n_ref, *, nf, final):
    f = pl.program_id(1)
    tm, D = o_ref.shape
    row_chunks = [slice(r, r + ROW_CHUNK) for r in range(0, tm, ROW_CHUNK)]

    def step(first, last):
        if first:
            n = jnp.concatenate([_rms(h_ref[rows, :], g_ref[...]).astype(BF16) for rows in row_chunks], axis=0)
            n_ref[...] = n
        else:
            n = n_ref[...]
        a = _dot(n, wg_ref[...])
        b = _dot(n, wu_ref[...].astype(BF16))
        hm = (_silu(a) * b).astype(BF16)
        for c in range(0, D, COL_CHUNK):
            cols = slice(c, c + COL_CHUNK)
            acc = _dot(hm, wd_ref[:, cols].astype(BF16))
            if not first:
                acc = o_ref[:, cols] + acc
            o_ref[:, cols] = h_ref[:, cols] + 0.5 * acc if last else acc
        if last and final:
            for rows in row_chunks:
                o_ref[rows, :] = _rms(o_ref[rows, :], fg_ref[...])

    assert nf > 2
    pl.when(f == 0)(functools.partial(step, True, False))
    pl.when((f > 0) & (f < nf - 1))(functools.partial(step, False, False))
    pl.when(f == nf - 1)(functools.partial(step, False, True))


def _ffn(h, g, wg, wu, wd, l, final_g, final):
    T, D = h.shape
    F = wg.shape[-1]
    tm = _tile(T, 1024)
    tf = _tile(F, 512)
    nf = F // tf
    return pl.pallas_call(
        functools.partial(_ffn_kernel, nf=nf, final=final),
        grid=(T // tm, nf),
        in_specs=[
            pl.BlockSpec((tm, D), lambda i, f: (i, 0)),
            pl.BlockSpec((None, 1, D), lambda i, f: (l, 0, 0)),
            pl.BlockSpec((None, D, tf), lambda i, f: (l, 0, f)),
            pl.BlockSpec((None, D, tf), lambda i, f: (l, 0, f)),
            pl.BlockSpec((None, tf, D), lambda i, f: (l, f, 0)),
            pl.BlockSpec((1, D), lambda i, f: (0, 0)),
        ],
        out_specs=pl.BlockSpec((tm, D), lambda i, f: (i, 0)),
        out_shape=jax.ShapeDtypeStruct((T, D), F32),
        scratch_shapes=[pltpu.VMEM((tm, D), BF16)],
        compiler_params=_cparams(("parallel", "arbitrary"), FFN_VMEM_LIMIT),
        name="ffn",
    )(h, g, wg, wu, wd, final_g)


def _resident(block_shape, index_map):
    return pl.BlockSpec(block_shape, index_map, pipeline_mode=pl.Buffered(1))


def _norm_mm_kernel(x_ref, g_ref, w_ref, o_ref):
    n = _rms(x_ref[...], g_ref[...]).astype(BF16)
    N = o_ref.shape[1]
    tn = min(N, COL_CHUNK)
    for c in range(N // tn):
        o_ref[:, c * tn:(c + 1) * tn] = _dot(n, w_ref[:, c * tn:(c + 1) * tn]).astype(o_ref.dtype)


def _norm_mm(x, g, w, l, out_dtype, tm_pref=512):
    M, K = x.shape
    N = w.shape[-1]
    tm = _tile(M, tm_pref)
    return pl.pallas_call(
        _norm_mm_kernel,
        grid=(M // tm,),
        in_specs=[
            pl.BlockSpec((tm, K), lambda i: (i, 0)),
            _resident((None, 1, K), lambda i: (l, 0, 0)),
            _resident((None, K, N), lambda i: (l, 0, 0)),
        ],
        out_specs=pl.BlockSpec((tm, N), lambda i: (i, 0)),
        out_shape=jax.ShapeDtypeStruct((M, N), out_dtype),
        compiler_params=_cparams(("parallel",)),
        name="norm_mm",
    )(x, g, w)


def _mm_res_kernel(*refs, ny):
    ys, ws, h_ref, o_ref = refs[:ny], refs[ny:2 * ny], refs[2 * ny], refs[2 * ny + 1]
    N = o_ref.shape[1]
    tn = min(N, COL_CHUNK)
    for c in range(N // tn):
        cols = slice(c * tn, (c + 1) * tn)
        acc = h_ref[:, cols]
        for y_ref, w_ref in zip(ys, ws):
            acc = acc + _dot(y_ref[...], w_ref[:, cols])
        o_ref[:, cols] = acc


def _mm_res(ys, w, l, h, tm_pref=512):
    T, N = h.shape
    tm = _tile(T, tm_pref)
    y_specs, w_specs, row = [], [], 0
    for y in ys:
        k = y.shape[1]
        assert row % k == 0
        y_specs.append(pl.BlockSpec((tm, k), lambda i: (i, 0)))
        w_specs.append(_resident((None, k, N), functools.partial(lambda i, rb: (l, rb, 0), rb=row // k)))
        row += k
    assert row == w.shape[1]
    return pl.pallas_call(
        functools.partial(_mm_res_kernel, ny=len(ys)),
        grid=(T // tm,),
        in_specs=y_specs + w_specs + [pl.BlockSpec((tm, N), lambda i: (i, 0))],
        out_specs=pl.BlockSpec((tm, N), lambda i: (i, 0)),
        out_shape=jax.ShapeDtypeStruct((T, N), F32),
        compiler_params=_cparams(("parallel",)),
        name="mm_res",
    )(*ys, *([w] * len(ys)), h)


def _sgu_kernel(u_ref, v_ref, g_ref, ws_ref, bt_ref, o_ref, *, nchunk):
    C = CHUNK
    row = lax.broadcasted_iota(jnp.int32, (C, C), 0)
    col = lax.broadcasted_iota(jnp.int32, (C, C), 1)
    wm = [jnp.where(row >= col, ws_ref[g], 0.0).astype(BF16) for g in range(A_GROUPS)]
    bt = bt_ref[...]
    for c in range(nchunk):
        rows = slice(c * C, (c + 1) * C)
        u = jax.nn.gelu(u_ref[rows, :])
        v = _rms(jax.nn.gelu(v_ref[rows, :]), g_ref[...]).astype(BF16)
        for g in range(A_GROUPS):
            cols = slice(g * C, (g + 1) * C)
            z = _dot(wm[g], v[:, cols]) + bt[:, g:g + 1]
            o_ref[rows, cols] = (u[:, cols] * z).astype(o_ref.dtype)


def _sgu(proj, g, ws, bt, l, width):
    T = proj.shape[0]
    ts = _tile(T, 4 * CHUNK)
    return pl.pallas_call(
        functools.partial(_sgu_kernel, nchunk=ts // CHUNK),
        grid=(T // ts,),
        in_specs=[
            pl.BlockSpec((ts, width), lambda i: (i, 0)),
            pl.BlockSpec((ts, width), lambda i: (i, 1)),
            pl.BlockSpec((None, 1, width), lambda i: (l, 0, 0)),
            pl.BlockSpec((None, A_GROUPS, CHUNK, CHUNK), lambda i: (l, 0, 0, 0)),
            pl.BlockSpec((None, CHUNK, A_GROUPS), lambda i: (l, 0, 0)),
        ],
        out_specs=pl.BlockSpec((ts, width), lambda i: (i, 0)),
        out_shape=jax.ShapeDtypeStruct((T, width), BF16),
        compiler_params=_cparams(("parallel",)),
        name="sgu",
    )(proj, proj, g, ws, bt)


def _ret_kernel(q_ref, k_ref, v_ref, gate_ref, cos_ref, sin_ref, dmask_ref, zeta_ref, xi_ref, gch_ref, gn_ref,
                o_ref, state_ref, *, nchunk):
    C = CHUNK
    hd = C
    half = hd // 2

    @pl.when(pl.program_id(1) == 0)
    def _():
        state_ref[...] = jnp.zeros_like(state_ref)

    for c in range(nchunk):
        rows = slice(c * C, (c + 1) * C)
        cos = cos_ref[rows, :]
        sin = sin_ref[rows, :]
        for h in range(R_HEADS):
            cols = slice(h * hd, (h + 1) * hd)
            q = q_ref[rows, cols]
            k = k_ref[rows, cols]
            q = q * cos + pltpu.roll(q, half, 1) * sin
            k = (k * cos + pltpu.roll(k, half, 1) * sin) * (hd ** -0.5)
            vb = v_ref[rows, cols].astype(BF16)
            scores = _dot_nt(q.astype(BF16), k.astype(BF16)) * dmask_ref[h]
            state = state_ref[h]
            y = _dot(scores.astype(BF16), vb) + _dot((q * xi_ref[:, cols]).astype(BF16), state.astype(BF16))
            kz = (k * zeta_ref[:, cols]).astype(BF16)
            state_ref[h] = gch_ref[h] * state + _dot(kz.T, vb)
            mu = jnp.mean(y, axis=-1, keepdims=True)
            yc = y - mu
            var = jnp.mean(yc * yc, axis=-1, keepdims=True)
            yn = yc * lax.rsqrt(var + EPS) * gn_ref[:, cols]
            o_ref[rows, cols] = (_silu(gate_ref[rows, cols]) * yn).astype(o_ref.dtype)


def _retention(proj, cos, sin, dmask, zeta, xi, gch, gn, l, B, S, width, col0):
    T = proj.shape[0]
    tr = _tile(S, 4 * CHUNK)
    ns = S // tr
    cb = col0 // width
    assert col0 % width == 0
    tok = lambda b, n: b * ns + n
    return pl.pallas_call(
        functools.partial(_ret_kernel, nchunk=tr // CHUNK),
        grid=(B, ns),
        in_specs=[
            pl.BlockSpec((tr, width), lambda b, n: (tok(b, n), cb)),
            pl.BlockSpec((tr, width), lambda b, n: (tok(b, n), cb + 1)),
            pl.BlockSpec((tr, width), lambda b, n: (tok(b, n), cb + 2)),
            pl.BlockSpec((tr, width), lambda b, n: (tok(b, n), cb + 3)),
            pl.BlockSpec((tr, CHUNK), lambda b, n: (n, 0)),
            pl.BlockSpec((tr, CHUNK), lambda b, n: (n, 0)),
            pl.BlockSpec((R_HEADS, CHUNK, CHUNK), lambda b, n: (0, 0, 0)),
            pl.BlockSpec((CHUNK, width), lambda b, n: (0, 0)),
            pl.BlockSpec((CHUNK, width), lambda b, n: (0, 0)),
            pl.BlockSpec((R_HEADS, 1, CHUNK), lambda b, n: (0, 0, 0)),
            pl.BlockSpec((None, 1, width), lambda b, n: (l, 0, 0)),
        ],
        out_specs=pl.BlockSpec((tr, width), lambda b, n: (tok(b, n), 0)),
        out_shape=jax.ShapeDtypeStruct((T, width), BF16),
        scratch_shapes=[pltpu.VMEM((R_HEADS, CHUNK, CHUNK), F32)],
        compiler_params=_cparams(("parallel", "arbitrary")),
        name="retention",
    )(proj, proj, proj, proj, cos, sin, dmask, zeta, xi, gch, gn)


def _retention_tables(S):
    C, H, dim = CHUNK, R_HEADS, CHUNK
    half = dim // 2
    pos = jnp.arange(S, dtype=F32)
    inv_freq = ROPE_BASE ** (-jnp.arange(half, dtype=F32) * 2.0 / dim)
    ang = pos[:, None] * inv_freq[None, :]
    cos, sin = jnp.cos(ang), jnp.sin(ang)
    cos_t = jnp.concatenate([cos, cos], axis=-1)
    sin_t = jnp.concatenate([-sin, sin], axis=-1)
    log_g = jnp.log1p(-jnp.exp2(-5.0 - jnp.arange(H, dtype=F32)))
    i = jnp.arange(C, dtype=F32)
    diff = i[:, None] - i[None, :]
    dmask = jnp.where(diff[None] >= 0, jnp.exp(jnp.maximum(diff, 0.0)[None] * log_g[:, None, None]), 0.0)
    zeta = jnp.exp((C - 1 - i)[None, :] * log_g[:, None]).T
    xi = jnp.exp((i + 1)[None, :] * log_g[:, None]).T
    gch = jnp.exp(C * log_g)
    rep = lambda t: jnp.repeat(t, dim, axis=1)
    return cos_t, sin_t, dmask, rep(zeta), rep(xi), jnp.broadcast_to(gch[:, None, None], (H, 1, C))


def _mla_proj_kernel(cq_ref, ckv_ref, kr_ref, qg_ref, kvg_ref, wq_ref, wkv_ref, rt_ref, q_ref, k_ref, v_ref):
    hq = MLA_NOPE + 2 * MLA_ROPE
    scale = (MLA_NOPE + MLA_ROPE) ** -0.5 * math.log2(math.e)
    lane = lax.broadcasted_iota(jnp.int32, (1, 2 * MLA_ROPE), 1)
    keep = jnp.where(lane < MLA_ROPE, scale, 0.0)
    rt = rt_ref[...]

    def rope(y):
        z = y * rt
        return z + pltpu.roll(z, MLA_ROPE, 1)

    x = kr_ref[...]
    half = MLA_ROPE // 2
    swapped = jnp.where(lane < half, pltpu.roll(x, 2 * MLA_ROPE - half, 1), pltpu.roll(x, half, 1))
    kr = jnp.where(lane < MLA_ROPE, x * rt + swapped * pltpu.roll(rt, MLA_ROPE, 1), 0.0).astype(BF16)
    q = _dot(_rms(cq_ref[...], qg_ref[...]).astype(BF16), wq_ref[...])
    kv = _dot(_rms(ckv_ref[...], kvg_ref[...]).astype(BF16), wkv_ref[...])
    for h in range(MLA_HEADS):
        q_ref[:, h * hq:h * hq + MLA_NOPE] = (q[:, h * hq:h * hq + MLA_NOPE] * scale).astype(BF16)
        q_ref[:, h * hq + MLA_NOPE:(h + 1) * hq] = (rope(q[:, h * hq + MLA_NOPE:(h + 1) * hq]) * keep).astype(BF16)
        k_ref[:, h * hq:h * hq + MLA_NOPE] = kv[:, h * MLA_NOPE:(h + 1) * MLA_NOPE].astype(BF16)
        k_ref[:, h * hq + MLA_NOPE:(h + 1) * hq] = kr
        v0 = MLA_HEADS * MLA_NOPE + h * MLA_V
        v_ref[:, 2 * h * MLA_V:(2 * h + 1) * MLA_V] = kv[:, v0:v0 + MLA_V].astype(BF16)
        v_ref[:, (2 * h + 1) * MLA_V:(2 * h + 2) * MLA_V] = jnp.ones((kv.shape[0], MLA_V), BF16)


def _mla_proj(proj, qg, kvg, wq, wkv, rt, l, S, q_lora, kv_lora, col0):
    T = proj.shape[0]
    tm = _tile(S, 512)
    ns = S // tm
    hq = MLA_NOPE + 2 * MLA_ROPE
    c_kv0 = col0 + q_lora
    c_kr0 = c_kv0 + kv_lora
    assert col0 % q_lora == 0 and c_kv0 % kv_lora == 0 and c_kr0 % LANES == 0
    return pl.pallas_call(
        _mla_proj_kernel,
        grid=(T // tm,),
        in_specs=[
            pl.BlockSpec((tm, q_lora), lambda i: (i, col0 // q_lora)),
            pl.BlockSpec((tm, kv_lora), lambda i: (i, c_kv0 // kv_lora)),
            pl.BlockSpec((tm, 2 * MLA_ROPE), lambda i: (i, c_kr0 // LANES)),
            pl.BlockSpec((None, 1, q_lora), lambda i: (l, 0, 0)),
            pl.BlockSpec((None, 1, kv_lora), lambda i: (l, 0, 0)),
            pl.BlockSpec((None, q_lora, MLA_HEADS * hq), lambda i: (l, 0, 0)),
            pl.BlockSpec((None, kv_lora, MLA_HEADS * (MLA_NOPE + MLA_V)), lambda i: (l, 0, 0)),
            pl.BlockSpec((tm, 2 * MLA_ROPE), lambda i: (i % ns, 0)),
        ],
        out_specs=[
            pl.BlockSpec((tm, MLA_HEADS * hq), lambda i: (i, 0)),
            pl.BlockSpec((tm, MLA_HEADS * hq), lambda i: (i, 0)),
            pl.BlockSpec((tm, MLA_HEADS * 2 * MLA_V), lambda i: (i, 0)),
        ],
        out_shape=[
            jax.ShapeDtypeStruct((T, MLA_HEADS * hq), BF16),
            jax.ShapeDtypeStruct((T, MLA_HEADS * hq), BF16),
            jax.ShapeDtypeStruct((T, MLA_HEADS * 2 * MLA_V), BF16),
        ],
        compiler_params=_cparams(("parallel",)),
        name="mla_proj",
    )(proj, proj, proj, qg, kvg, wq, wkv, rt)


FLASH_HEADS = 2


def _flash_kernel(q_ref, k_ref, v_ref, o_ref, s_ref, mc_ref, m_ref, acc_ref, *, tq):
    qi = pl.program_id(2)
    nq = k_ref.shape[0] // tq
    rep = tq // LANES
    hq = MLA_NOPE + 2 * MLA_ROPE
    heads = range(FLASH_HEADS)
    q = [q_ref[:, g * hq:(g + 1) * hq] for g in heads]

    def keys(j):
        return slice(j * tq, (j + 1) * tq)

    def rowmax(s):
        return jnp.broadcast_to(jnp.max(s, axis=-1, keepdims=True), (tq, LANES))

    def scores(g, j, slot):
        s = _dot_nt(q[g], k_ref[keys(j), g * hq:(g + 1) * hq])
        s_ref[g, slot] = s
        mc_ref[g, slot] = rowmax(s)

    def update(g, j, slot, diagonal=False):
        s = s_ref[g, slot]
        if diagonal:
            row = lax.broadcasted_iota(jnp.int32, s.shape, 0)
            col = lax.broadcasted_iota(jnp.int32, s.shape, 1)
            s = jnp.where(col <= row, s, -1e30)
            m_cur = rowmax(s)
        else:
            m_cur = mc_ref[g, slot]
        m = m_ref[g]
        m_new = jnp.maximum(m, m_cur)
        m_ref[g] = m_new
        p = jnp.exp2(s - jnp.tile(m_new, (1, rep)))
        alpha = jnp.exp2(m - m_new)
        pv = _dot(p.astype(BF16), v_ref[keys(j), 2 * g * MLA_V:2 * (g + 1) * MLA_V])
        acc_ref[g] = jnp.tile(alpha, (1, 2)) * acc_ref[g] + pv

    def tile(c):
        for g in heads:
            m_ref[g] = jnp.full((tq, LANES), -1e30, F32)
            acc_ref[g] = jnp.zeros((tq, 2 * MLA_V), F32)
            scores(g, 0, 0)
        for j in range(c):
            for g in heads:
                scores(g, j + 1, (j + 1) % 2)
                update(g, j, j % 2)
        for g in heads:
            update(g, c, c % 2, diagonal=True)
            o_ref[:, g * MLA_V:(g + 1) * MLA_V] = (acc_ref[g, :, :MLA_V] / acc_ref[g, :, MLA_V:]).astype(o_ref.dtype)

    for c in range(nq):
        pl.when(qi == c)(functools.partial(tile, c))


def _flash(q, k, v, B, S):
    T = q.shape[0]
    tq = _tile(S, 512)
    nq = S // tq
    G = FLASH_HEADS
    hq = MLA_NOPE + 2 * MLA_ROPE
    return pl.pallas_call(
        functools.partial(_flash_kernel, tq=tq),
        grid=(B, MLA_HEADS // G, nq),
        in_specs=[
            pl.BlockSpec((tq, G * hq), lambda b, h, i: (b * nq + i, h)),
            pl.BlockSpec((S, G * hq), lambda b, h, i: (b, h)),
            pl.BlockSpec((S, G * 2 * MLA_V), lambda b, h, i: (b, h)),
        ],
        out_specs=pl.BlockSpec((tq, G * MLA_V), lambda b, h, i: (b * nq + i, h)),
        out_shape=jax.ShapeDtypeStruct((T, MLA_HEADS * MLA_V), BF16),
        scratch_shapes=[
            pltpu.VMEM((G, 2, tq, tq), F32),
            pltpu.VMEM((G, 2, tq, LANES), F32),
            pltpu.VMEM((G, tq, LANES), F32),
            pltpu.VMEM((G, tq, 2 * MLA_V), F32),
        ],
        compiler_params=_cparams(("parallel", "parallel", "arbitrary")),
        name="mla_flash",
    )(q, k, v)


def _mla_rope_table(S):
    half = MLA_ROPE // 2
    pos = jnp.arange(S, dtype=F32)
    inv_freq = ROPE_BASE ** (-jnp.arange(half, dtype=F32) * 2.0 / MLA_ROPE)
    ang = pos[:, None] * inv_freq[None, :]
    cos, sin = jnp.cos(ang), jnp.sin(ang)
    return jnp.concatenate([cos, cos, -sin, sin], axis=-1)


def _swap_halves(w):
    half = w.shape[-1] // 2
    return jnp.concatenate([w[..., half:], w[..., :half]], axis=-1)


def _xattn_kernel(h_ref, g_ref, wq_ref, k_ref, v_ref, wo_ref, o_ref, q_ref, s_ref, a_ref, *, hd):
    D = o_ref.shape[1]
    scale = hd ** -0.5
    heads = [slice(h * hd, (h + 1) * hd) for h in range(XA_HEADS)]
    col_chunks = [slice(c, c + COL_CHUNK) for c in range(0, D, COL_CHUNK)]
    n = _rms(h_ref[...], g_ref[...]).astype(BF16)
    for cols in col_chunks:
        q_ref[:, cols] = _dot(n, wq_ref[:, cols]).astype(BF16)
    for h, cols in enumerate(heads):
        s_ref[h] = _dot_nt(q_ref[:, cols], k_ref[:, cols]) * scale
    for h, cols in enumerate(heads):
        s = s_ref[h]
        e = jnp.exp(s - jnp.max(s, axis=-1, keepdims=True))
        p = e / jnp.sum(e, axis=-1, keepdims=True)
        a_ref[:, cols] = _dot(p.astype(BF16), v_ref[:, cols]).astype(BF16)
    for cols in col_chunks:
        o_ref[:, cols] = h_ref[:, cols] + _dot(a_ref[...], wo_ref[:, cols])


def _xattn(h, g, wq, kv, wo, l, S, M):
    T, D = h.shape
    tm = _tile(S, 512)
    ns = S // tm
    return pl.pallas_call(
        functools.partial(_xattn_kernel, hd=D // XA_HEADS),
        grid=(T // tm,),
        in_specs=[
            pl.BlockSpec((tm, D), lambda i: (i, 0)),
            _resident((None, 1, D), lambda i: (l, 0, 0)),
            _resident((None, D, D), lambda i: (l, 0, 0)),
            pl.BlockSpec((M, D), lambda i: (i // ns, 0)),
            pl.BlockSpec((M, D), lambda i: (i // ns, 1)),
            _resident((None, D, D), lambda i: (l, 0, 0)),
        ],
        out_specs=pl.BlockSpec((tm, D), lambda i: (i, 0)),
        out_shape=jax.ShapeDtypeStruct((T, D), F32),
        scratch_shapes=[pltpu.VMEM((tm, D), BF16), pltpu.VMEM((XA_HEADS, tm, M), F32), pltpu.VMEM((tm, D), BF16)],
        compiler_params=_cparams(("parallel",)),
        name="xattn",
    )(h, g, wq, kv, kv, wo)


def kernel(x, mem, ffn1_norm, ffn1_w_gate, ffn1_w_up, ffn1_w_down, mix_norm, w_in, sgu_norm, sgu_w_s, sgu_b, ret_gn, q_norm, w_uq, kv_norm, w_ukv, w_out, xa_norm, mem_norm, xa_wq, xa_wkv, xa_wo, ffn2_norm, ffn2_w_gate, ffn2_w_up, ffn2_w_down, final_norm):
    B, S, D = x.shape
    M = mem.shape[1]
    L = w_in.shape[0]
    T = B * S
    a_width = sgu_norm.shape[-1]
    r_width = ret_gn.shape[-1]
    q_lora = q_norm.shape[-1]
    kv_lora = kv_norm.shape[-1]
    assert a_width == A_GROUPS * CHUNK and r_width == R_HEADS * CHUNK and S % CHUNK == 0
    ret_col0 = 2 * a_width
    mla_col0 = ret_col0 + 4 * r_width
    kr_col0 = mla_col0 + q_lora + kv_lora
    assert w_in.shape[-1] == kr_col0 + MLA_ROPE

    bf = lambda w: w.astype(BF16)
    row = lambda g: g.reshape(g.shape[0], 1, g.shape[1])

    n_in = kr_col0 + MLA_ROPE
    n_in_pad = -(-n_in // COL_CHUNK) * COL_CHUNK
    w_in_x = bf(jnp.pad(w_in, ((0, 0), (0, 0), (0, n_in_pad - n_in))))
    wq4 = w_uq.reshape(L, q_lora, MLA_HEADS, MLA_NOPE + MLA_ROPE)
    wq_x = bf(jnp.concatenate([wq4, _swap_halves(wq4[..., MLA_NOPE:])], axis=-1)).reshape(L, q_lora, -1)
    wkv4 = w_ukv.reshape(L, kv_lora, MLA_HEADS, MLA_NOPE + MLA_V)
    wkv_x = bf(jnp.concatenate([wkv4[..., :MLA_NOPE].reshape(L, kv_lora, -1),
                                wkv4[..., MLA_NOPE:].reshape(L, kv_lora, -1)], axis=-1))
    sgu_bt = jnp.swapaxes(sgu_b, 1, 2)
    ret_tabs = _retention_tables(S)
    mla_rt = _mla_rope_table(S)
    w1g, w1u, w1d = bf(ffn1_w_gate), ffn1_w_up, ffn1_w_down
    w2g, w2u, w2d = bf(ffn2_w_gate), ffn2_w_up, ffn2_w_down
    w_out_b, xa_wq_b, xa_wkv_b, xa_wo_b = bf(w_out), bf(xa_wq), bf(xa_wkv), bf(xa_wo)
    fin = final_norm.reshape(1, D)

    h = x.reshape(T, D)
    mem2 = mem.reshape(B * M, D)
    for l in range(L):
        h = _ffn(h, row(ffn1_norm), w1g, w1u, w1d, l, fin, False)
        proj = _norm_mm(h, row(mix_norm), w_in_x, l, F32)
        y_a = _sgu(proj, row(sgu_norm), sgu_w_s, sgu_bt, l, a_width)
        y_r = _retention(proj, *ret_tabs, row(ret_gn), l, B, S, r_width, ret_col0)
        q, k, v = _mla_proj(proj, row(q_norm), row(kv_norm), wq_x, wkv_x, mla_rt, l, S, q_lora, kv_lora, mla_col0)
        y_c = _flash(q, k, v, B, S)
        h = _mm_res([y_a, y_r, y_c], w_out_b, l, h)
        kvm = _norm_mm(mem2, row(mem_norm), xa_wkv_b, l, BF16)
        h = _xattn(h, row(xa_norm), xa_wq_b, kvm, xa_wo_b, l, S, M)
        h = _ffn(h, row(ffn2_norm), w2g, w2u, w2d, l, fin, l == L - 1)
    return h.reshape(B, S, D)
```

```python
import functools
import math

import jax
import jax.numpy as jnp
from jax import lax
from jax.experimental import pallas as pl
from jax.experimental.pallas import tpu as pltpu

F32 = jnp.float32
BF16 = jnp.bfloat16

EPS = 1e-6
ROPE_BASE = 10000.0
CHUNK = 128
A_GROUPS = 4
R_HEADS = 4
MLA_HEADS = 8
MLA_NOPE = 128
MLA_ROPE = 64
MLA_V = 128
XA_HEADS = 4
LANES = 128

VMEM_LIMIT = 56 * 1024 * 1024
FFN_VMEM_LIMIT = 62 * 1024 * 1024


def _cparams(sem, vmem_limit=VMEM_LIMIT):
    return pltpu.CompilerParams(dimension_semantics=sem, vmem_limit_bytes=vmem_limit)


def _tile(n, pref):
    t = min(n, pref)
    assert n % t == 0, (n, t)
    return t


def _rms(x, g):
    return x * lax.rsqrt(jnp.mean(x * x, axis=-1, keepdims=True) + EPS) * g


def _silu(x):
    return x * (1.0 / (1.0 + jnp.exp(-x)))


def _dot(a, b):
    return jnp.dot(a, b, preferred_element_type=F32)


def _dot_nt(a, b):
    return lax.dot_general(a, b, (((1,), (1,)), ((), ())), preferred_element_type=F32)


COL_CHUNK = 1024
ROW_CHUNK = 256


def _ffn_kernel(h_ref, g_ref, wg_ref, wu_ref, wd_ref, fg_ref, o_ref, n_ref, *, nf, final):
    f = pl.program_id(1)
    tm, D = o_ref.shape
    row_chunks = [slice(r, r + ROW_CHUNK) for r in range(0, tm, ROW_CHUNK)]

    def step(first, last):
        if first:
            n = jnp.concatenate([_rms(h_ref[rows, :], g_ref[...]).astype(BF16) for rows in row_chunks], axis=0)
            n_ref[...] = n
        else:
            n = n_ref[...]
        a = _dot(n, wg_ref[...])
        b = _dot(n, wu_ref[...].astype(BF16))
        hm = (_silu(a) * b).astype(BF16)
        for c in range(0, D, COL_CHUNK):
            cols = slice(c, c + COL_CHUNK)
            acc = _dot(hm, wd_ref[:, cols].astype(BF16))
            if not first:
                acc = o_ref[:, cols] + acc
            o_ref[:, cols] = h_ref[:, cols] + 0.5 * acc if last else acc
        if last and final:
            for rows in row_chunks:
                o_ref[rows, :] = _rms(o_ref[rows, :], fg_ref[...])

    assert nf > 2
    pl.when(f == 0)(functools.partial(step, True, False))
    pl.when((f > 0) & (f < nf - 1))(functools.partial(step, False, False))
    pl.when(f == nf - 1)(functools.partial(step, False, True))


def _ffn(h, g, wg, wu, wd, l, final_g, final):
    T, D = h.shape
    F = wg.shape[-1]
    tm = _tile(T, 1024)
    tf = _tile(F, 512)
    nf = F // tf
    return pl.pallas_call(
        functools.partial(_ffn_kernel, nf=nf, final=final),
        grid=(T // tm, nf),
        in_specs=[
            pl.BlockSpec((tm, D), lambda i, f: (i, 0)),
            pl.BlockSpec((None, 1, D), lambda i, f: (l, 0, 0)),
            pl.BlockSpec((None, D, tf), lambda i, f: (l, 0, f)),
            pl.BlockSpec((None, D, tf), lambda i, f: (l, 0, f)),
            pl.BlockSpec((None, tf, D), lambda i, f: (l, f, 0)),
            pl.BlockSpec((1, D), lambda i, f: (0, 0)),
        ],
        out_specs=pl.BlockSpec((tm, D), lambda i, f: (i, 0)),
        out_shape=jax.ShapeDtypeStruct((T, D), F32),
        scratch_shapes=[pltpu.VMEM((tm, D), BF16)],
        compiler_params=_cparams(("parallel", "arbitrary"), FFN_VMEM_LIMIT),
        name="ffn",
    )(h, g, wg, wu, wd, final_g)


def _resident(block_shape, index_map):
    return pl.BlockSpec(block_shape, index_map, pipeline_mode=pl.Buffered(1))


def _norm_mm_kernel(x_ref, g_ref, w_ref, o_ref):
    n = _rms(x_ref[...], g_ref[...]).astype(BF16)
    N = o_ref.shape[1]
    tn = min(N, COL_CHUNK)
    for c in range(N // tn):
        w = w_ref[:, c * tn:(c + 1) * tn].astype(BF16)
        o_ref[:, c * tn:(c + 1) * tn] = _dot(n, w).astype(o_ref.dtype)


def _norm_mm(x, g, w, l, out_dtype, tm_pref=512):
    M, K = x.shape
    N = w.shape[-1]
    tm = _tile(M, tm_pref)
    return pl.pallas_call(
        _norm_mm_kernel,
        grid=(M // tm,),
        in_specs=[
            pl.BlockSpec((tm, K), lambda i: (i, 0)),
            _resident((None, 1, K), lambda i: (l, 0, 0)),
            _resident((None, K, N), lambda i: (l, 0, 0)),
        ],
        out_specs=pl.BlockSpec((tm, N), lambda i: (i, 0)),
        out_shape=jax.ShapeDtypeStruct((M, N), out_dtype),
        compiler_params=_cparams(("parallel",)),
        name="norm_mm",
    )(x, g, w)


def _mm_res_kernel(*refs, ny):
    ys, ws, h_ref, o_ref = refs[:ny], refs[ny:2 * ny], refs[2 * ny], refs[2 * ny + 1]
    N = o_ref.shape[1]
    tn = min(N, COL_CHUNK)
    for c in range(N // tn):
        cols = slice(c * tn, (c + 1) * tn)
        acc = h_ref[:, cols]
        for y_ref, w_ref in zip(ys, ws):
            acc = acc + _dot(y_ref[...], w_ref[:, cols].astype(BF16))
        o_ref[:, cols] = acc


def _mm_res(ys, w, l, h, tm_pref=512):
    T, N = h.shape
    tm = _tile(T, tm_pref)
    y_specs, w_specs, row = [], [], 0
    for y in ys:
        k = y.shape[1]
        assert row % k == 0
        y_specs.append(pl.BlockSpec((tm, k), lambda i: (i, 0)))
        w_specs.append(_resident((None, k, N), functools.partial(lambda i, rb: (l, rb, 0), rb=row // k)))
        row += k
    assert row == w.shape[1]
    return pl.pallas_call(
        functools.partial(_mm_res_kernel, ny=len(ys)),
        grid=(T // tm,),
        in_specs=y_specs + w_specs + [pl.BlockSpec((tm, N), lambda i: (i, 0))],
        out_specs=pl.BlockSpec((tm, N), lambda i: (i, 0)),
        out_shape=jax.ShapeDtypeStruct((T, N), F32),
        compiler_params=_cparams(("parallel",)),
        name="mm_res",
    )(*ys, *([w] * len(ys)), h)


def _sgu_kernel(u_ref, v_ref, g_ref, ws_ref, bt_ref, o_ref, *, nchunk):
    C = CHUNK
    row = lax.broadcasted_iota(jnp.int32, (C, C), 0)
    col = lax.broadcasted_iota(jnp.int32, (C, C), 1)
    wm = [jnp.where(row >= col, ws_ref[g], 0.0).astype(BF16) for g in range(A_GROUPS)]
    bt = bt_ref[...]
    for c in range(nchunk):
        rows = slice(c * C, (c + 1) * C)
        u = jax.nn.gelu(u_ref[rows, :])
        v = _rms(jax.nn.gelu(v_ref[rows, :]), g_ref[...]).astype(BF16)
        for g in range(A_GROUPS):
            cols = slice(g * C, (g + 1) * C)
            z = _dot(wm[g], v[:, cols]) + bt[:, g:g + 1]
            o_ref[rows, cols] = (u[:, cols] * z).astype(o_ref.dtype)


def _sgu(proj, g, ws, bt, l, width):
    T = proj.shape[0]
    ts = _tile(T, 4 * CHUNK)
    return pl.pallas_call(
        functools.partial(_sgu_kernel, nchunk=ts // CHUNK),
        grid=(T // ts,),
        in_specs=[
            pl.BlockSpec((ts, width), lambda i: (i, 0)),
            pl.BlockSpec((ts, width), lambda i: (i, 1)),
            pl.BlockSpec((None, 1, width), lambda i: (l, 0, 0)),
            pl.BlockSpec((None, A_GROUPS, CHUNK, CHUNK), lambda i: (l, 0, 0, 0)),
            pl.BlockSpec((None, CHUNK, A_GROUPS), lambda i: (l, 0, 0)),
        ],
        out_specs=pl.BlockSpec((ts, width), lambda i: (i, 0)),
        out_shape=jax.ShapeDtypeStruct((T, width), BF16),
        compiler_params=_cparams(("parallel",)),
        name="sgu",
    )(proj, proj, g, ws, bt)


def _ret_kernel(q_ref, k_ref, v_ref, gate_ref, cos_ref, sin_ref, dmask_ref, zeta_ref, xi_ref, gch_ref, gn_ref,
                o_ref, state_ref, *, nchunk):
    C = CHUNK
    hd = C
    half = hd // 2

    @pl.when(pl.program_id(1) == 0)
    def _():
        state_ref[...] = jnp.zeros_like(state_ref)

    for c in range(nchunk):
        rows = slice(c * C, (c + 1) * C)
        cos = cos_ref[rows, :]
        sin = sin_ref[rows, :]
        for h in range(R_HEADS):
            cols = slice(h * hd, (h + 1) * hd)
            q = q_ref[rows, cols]
            k = k_ref[rows, cols]
            q = q * cos + pltpu.roll(q, half, 1) * sin
            k = (k * cos + pltpu.roll(k, half, 1) * sin) * (hd ** -0.5)
            vb = v_ref[rows, cols].astype(BF16)
            scores = _dot_nt(q.astype(BF16), k.astype(BF16)) * dmask_ref[h]
            state = state_ref[h]
            y = _dot(scores.astype(BF16), vb) + _dot((q * xi_ref[:, cols]).astype(BF16), state.astype(BF16))
            kz = (k * zeta_ref[:, cols]).astype(BF16)
            state_ref[h] = gch_ref[h] * state + _dot(kz.T, vb)
            mu = jnp.mean(y, axis=-1, keepdims=True)
            yc = y - mu
            var = jnp.mean(yc * yc, axis=-1, keepdims=True)
            yn = yc * lax.rsqrt(var + EPS) * gn_ref[:, cols]
            o_ref[rows, cols] = (_silu(gate_ref[rows, cols]) * yn).astype(o_ref.dtype)


def _retention(proj, cos, sin, dmask, zeta, xi, gch, gn, l, B, S, width, col0):
    T = proj.shape[0]
    tr = _tile(S, 4 * CHUNK)
    ns = S // tr
    cb = col0 // width
    assert col0 % width == 0
    tok = lambda b, n: b * ns + n
    return pl.pallas_call(
        functools.partial(_ret_kernel, nchunk=tr // CHUNK),
        grid=(B, ns),
        in_specs=[
            pl.BlockSpec((tr, width), lambda b, n: (tok(b, n), cb)),
            pl.BlockSpec((tr, width), lambda b, n: (tok(b, n), cb + 1)),
            pl.BlockSpec((tr, width), lambda b, n: (tok(b, n), cb + 2)),
            pl.BlockSpec((tr, width), lambda b, n: (tok(b, n), cb + 3)),
            pl.BlockSpec((tr, CHUNK), lambda b, n: (n, 0)),
            pl.BlockSpec((tr, CHUNK), lambda b, n: (n, 0)),
            pl.BlockSpec((R_HEADS, CHUNK, CHUNK), lambda b, n: (0, 0, 0)),
            pl.BlockSpec((CHUNK, width), lambda b, n: (0, 0)),
            pl.BlockSpec((CHUNK, width), lambda b, n: (0, 0)),
            pl.BlockSpec((R_HEADS, 1, CHUNK), lambda b, n: (0, 0, 0)),
            pl.BlockSpec((None, 1, width), lambda b, n: (l, 0, 0)),
        ],
        out_specs=pl.BlockSpec((tr, width), lambda b, n: (tok(b, n), 0)),
        out_shape=jax.ShapeDtypeStruct((T, width), BF16),
        scratch_shapes=[pltpu.VMEM((R_HEADS, CHUNK, CHUNK), F32)],
        compiler_params=_cparams(("parallel", "arbitrary")),
        name="retention",
    )(proj, proj, proj, proj, cos, sin, dmask, zeta, xi, gch, gn)


def _retention_tables(S):
    C, H, dim = CHUNK, R_HEADS, CHUNK
    half = dim // 2
    pos = jnp.arange(S, dtype=F32)
    inv_freq = ROPE_BASE ** (-jnp.arange(half, dtype=F32) * 2.0 / dim)
    ang = pos[:, None] * inv_freq[None, :]
    cos, sin = jnp.cos(ang), jnp.sin(ang)
    cos_t = jnp.concatenate([cos, cos], axis=-1)
    sin_t = jnp.concatenate([-sin, sin], axis=-1)
    log_g = jnp.log1p(-jnp.exp2(-5.0 - jnp.arange(H, dtype=F32)))
    i = jnp.arange(C, dtype=F32)
    diff = i[:, None] - i[None, :]
    dmask = jnp.where(diff[None] >= 0, jnp.exp(jnp.maximum(diff, 0.0)[None] * log_g[:, None, None]), 0.0)
    zeta = jnp.exp((C - 1 - i)[None, :] * log_g[:, None]).T
    xi = jnp.exp((i + 1)[None, :] * log_g[:, None]).T
    gch = jnp.exp(C * log_g)
    rep = lambda t: jnp.repeat(t, dim, axis=1)
    return cos_t, sin_t, dmask, rep(zeta), rep(xi), jnp.broadcast_to(gch[:, None, None], (H, 1, C))


def _mla_proj_kernel(cq_ref, ckv_ref, kr_ref, qg_ref, kvg_ref, wq_ref, wkv_ref, rt_ref, q_ref, k_ref, v_ref):
    hq = MLA_NOPE + 2 * MLA_ROPE
    scale = (MLA_NOPE + MLA_ROPE) ** -0.5 * math.log2(math.e)
    lane = lax.broadcasted_iota(jnp.int32, (1, 2 * MLA_ROPE), 1)
    keep = jnp.where(lane < MLA_ROPE, scale, 0.0)
    rt = rt_ref[...]

    def rope(y):
        z = y * rt
        return z + pltpu.roll(z, MLA_ROPE, 1)

    x = kr_ref[...]
    half = MLA_ROPE // 2
    swapped = jnp.where(lane < half, pltpu.roll(x, 2 * MLA_ROPE - half, 1), pltpu.roll(x, half, 1))
    kr = jnp.where(lane < MLA_ROPE, x * rt + swapped * pltpu.roll(rt, MLA_ROPE, 1), 0.0).astype(BF16)
    q = _dot(_rms(cq_ref[...], qg_ref[...]).astype(BF16), wq_ref[...])
    kv = _dot(_rms(ckv_ref[...], kvg_ref[...]).astype(BF16), wkv_ref[...])
    for h in range(MLA_HEADS):
        q_ref[:, h * hq:h * hq + MLA_NOPE] = (q[:, h * hq:h * hq + MLA_NOPE] * scale).astype(BF16)
        q_ref[:, h * hq + MLA_NOPE:(h + 1) * hq] = (rope(q[:, h * hq + MLA_NOPE:(h + 1) * hq]) * keep).astype(BF16)
        k_ref[:, h * hq:h * hq + MLA_NOPE] = kv[:, h * MLA_NOPE:(h + 1) * MLA_NOPE].astype(BF16)
        k_ref[:, h * hq + MLA_NOPE:(h + 1) * hq] = kr
        v0 = MLA_HEADS * MLA_NOPE + h * MLA_V
        v_ref[:, 2 * h * MLA_V:(2 * h + 1) * MLA_V] = kv[:, v0:v0 + MLA_V].astype(BF16)
        v_ref[:, (2 * h + 1) * MLA_V:(2 * h + 2) * MLA_V] = jnp.ones((kv.shape[0], MLA_V), BF16)


def _mla_proj(proj, qg, kvg, wq, wkv, rt, l, S, q_lora, kv_lora, col0):
    T = proj.shape[0]
    tm = _tile(S, 512)
    ns = S // tm
    hq = MLA_NOPE + 2 * MLA_ROPE
    c_kv0 = col0 + q_lora
    c_kr0 = c_kv0 + kv_lora
    assert col0 % q_lora == 0 and c_kv0 % kv_lora == 0 and c_kr0 % LANES == 0
    return pl.pallas_call(
        _mla_proj_kernel,
        grid=(T // tm,),
        in_specs=[
            pl.BlockSpec((tm, q_lora), lambda i: (i, col0 // q_lora)),
            pl.BlockSpec((tm, kv_lora), lambda i: (i, c_kv0 // kv_lora)),
            pl.BlockSpec((tm, 2 * MLA_ROPE), lambda i: (i, c_kr0 // LANES)),
            pl.BlockSpec((None, 1, q_lora), lambda i: (l, 0, 0)),
            pl.BlockSpec((None, 1, kv_lora), lambda i: (l, 0, 0)),
            pl.BlockSpec((None, q_lora, MLA_HEADS * hq), lambda i: (l, 0, 0)),
            pl.BlockSpec((None, kv_lora, MLA_HEADS * (MLA_NOPE + MLA_V)), lambda i: (l, 0, 0)),
            pl.BlockSpec((tm, 2 * MLA_ROPE), lambda i: (i % ns, 0)),
        ],
        out_specs=[
            pl.BlockSpec((tm, MLA_HEADS * hq), lambda i: (i, 0)),
            pl.BlockSpec((tm, MLA_HEADS * hq), lambda i: (i, 0)),
            pl.BlockSpec((tm, MLA_HEADS * 2 * MLA_V), lambda i: (i, 0)),
        ],
        out_shape=[
            jax.ShapeDtypeStruct((T, MLA_HEADS * hq), BF16),
            jax.ShapeDtypeStruct((T, MLA_HEADS * hq), BF16),
            jax.ShapeDtypeStruct((T, MLA_HEADS * 2 * MLA_V), BF16),
        ],
        compiler_params=_cparams(("parallel",)),
        name="mla_proj",
    )(proj, proj, proj, qg, kvg, wq, wkv, rt)


FLASH_HEADS = 2


def _flash_kernel(q_ref, k_ref, v_ref, o_ref, s_ref, mc_ref, m_ref, acc_ref, *, tq):
    qi = pl.program_id(2)
    nq = k_ref.shape[0] // tq
    rep = tq // LANES
    hq = MLA_NOPE + 2 * MLA_ROPE
    heads = range(FLASH_HEADS)
    q = [q_ref[:, g * hq:(g + 1) * hq] for g in heads]

    def keys(j):
        return slice(j * tq, (j + 1) * tq)

    def rowmax(s):
        return jnp.broadcast_to(jnp.max(s, axis=-1, keepdims=True), (tq, LANES))

    def scores(g, j, slot):
        s = _dot_nt(q[g], k_ref[keys(j), g * hq:(g + 1) * hq])
        s_ref[g, slot] = s
        mc_ref[g, slot] = rowmax(s)

    def update(g, j, slot, diagonal=False):
        s = s_ref[g, slot]
        if diagonal:
            row = lax.broadcasted_iota(jnp.int32, s.shape, 0)
            col = lax.broadcasted_iota(jnp.int32, s.shape, 1)
            s = jnp.where(col <= row, s, -1e30)
            m_cur = rowmax(s)
        else:
            m_cur = mc_ref[g, slot]
        m = m_ref[g]
        m_new = jnp.maximum(m, m_cur)
        m_ref[g] = m_new
        p = jnp.exp2(s - jnp.tile(m_new, (1, rep)))
        alpha = jnp.exp2(m - m_new)
        pv = _dot(p.astype(BF16), v_ref[keys(j), 2 * g * MLA_V:2 * (g + 1) * MLA_V])
        acc_ref[g] = jnp.tile(alpha, (1, 2)) * acc_ref[g] + pv

    def tile(c):
        for g in heads:
            m_ref[g] = jnp.full((tq, LANES), -1e30, F32)
            acc_ref[g] = jnp.zeros((tq, 2 * MLA_V), F32)
            scores(g, 0, 0)
        for j in range(c):
            for g in heads:
                scores(g, j + 1, (j + 1) % 2)
                update(g, j, j % 2)
        for g in heads:
            update(g, c, c % 2, diagonal=True)
            o_ref[:, g * MLA_V:(g + 1) * MLA_V] = (acc_ref[g, :, :MLA_V] / acc_ref[g, :, MLA_V:]).astype(o_ref.dtype)

    for c in range(nq):
        pl.when(qi == c)(functools.partial(tile, c))


def _flash(q, k, v, B, S):
    T = q.shape[0]
    tq = _tile(S, 512)
    nq = S // tq
    G = FLASH_HEADS
    hq = MLA_NOPE + 2 * MLA_ROPE
    return pl.pallas_call(
        functools.partial(_flash_kernel, tq=tq),
        grid=(B, MLA_HEADS // G, nq),
        in_specs=[
            pl.BlockSpec((tq, G * hq), lambda b, h, i: (b * nq + i, h)),
            pl.BlockSpec((S, G * hq), lambda b, h, i: (b, h)),
            pl.BlockSpec((S, G * 2 * MLA_V), lambda b, h, i: (b, h)),
        ],
        out_specs=pl.BlockSpec((tq, G * MLA_V), lambda b, h, i: (b * nq + i, h)),
        out_shape=jax.ShapeDtypeStruct((T, MLA_HEADS * MLA_V), BF16),
        scratch_shapes=[
            pltpu.VMEM((G, 2, tq, tq), F32),
            pltpu.VMEM((G, 2, tq, LANES), F32),
            pltpu.VMEM((G, tq, LANES), F32),
            pltpu.VMEM((G, tq, 2 * MLA_V), F32),
        ],
        compiler_params=_cparams(("parallel", "parallel", "arbitrary")),
        name="mla_flash",
    )(q, k, v)


def _mla_rope_table(S):
    half = MLA_ROPE // 2
    pos = jnp.arange(S, dtype=F32)
    inv_freq = ROPE_BASE ** (-jnp.arange(half, dtype=F32) * 2.0 / MLA_ROPE)
    ang = pos[:, None] * inv_freq[None, :]
    cos, sin = jnp.cos(ang), jnp.sin(ang)
    return jnp.concatenate([cos, cos, -sin, sin], axis=-1)


def _swap_halves(w):
    half = w.shape[-1] // 2
    return jnp.concatenate([w[..., half:], w[..., :half]], axis=-1)


def _xattn_kernel(h_ref, g_ref, wq_ref, k_ref, v_ref, wo_ref, o_ref, q_ref, s_ref, a_ref, *, hd):
    D = o_ref.shape[1]
    scale = hd ** -0.5
    heads = [slice(h * hd, (h + 1) * hd) for h in range(XA_HEADS)]
    col_chunks = [slice(c, c + COL_CHUNK) for c in range(0, D, COL_CHUNK)]
    n = _rms(h_ref[...], g_ref[...]).astype(BF16)
    for cols in col_chunks:
        q_ref[:, cols] = _dot(n, wq_ref[:, cols]).astype(BF16)
    for h, cols in enumerate(heads):
        s_ref[h] = _dot_nt(q_ref[:, cols], k_ref[:, cols]) * scale
    for h, cols in enumerate(heads):
        s = s_ref[h]
        e = jnp.exp(s - jnp.max(s, axis=-1, keepdims=True))
        p = e / jnp.sum(e, axis=-1, keepdims=True)
        a_ref[:, cols] = _dot(p.astype(BF16), v_ref[:, cols]).astype(BF16)
    for cols in col_chunks:
        o_ref[:, cols] = h_ref[:, cols] + _dot(a_ref[...], wo_ref[:, cols])


def _xattn(h, g, wq, kv, wo, l, S, M):
    T, D = h.shape
    tm = _tile(S, 512)
    ns = S // tm
    return pl.pallas_call(
        functools.partial(_xattn_kernel, hd=D // XA_HEADS),
        grid=(T // tm,),
        in_specs=[
            pl.BlockSpec((tm, D), lambda i: (i, 0)),
            _resident((None, 1, D), lambda i: (l, 0, 0)),
            _resident((None, D, D), lambda i: (l, 0, 0)),
            pl.BlockSpec((M, D), lambda i: (i // ns, 0)),
            pl.BlockSpec((M, D), lambda i: (i // ns, 1)),
            _resident((None, D, D), lambda i: (l, 0, 0)),
        ],
        out_specs=pl.BlockSpec((tm, D), lambda i: (i, 0)),
        out_shape=jax.ShapeDtypeStruct((T, D), F32),
        scratch_shapes=[pltpu.VMEM((tm, D), BF16), pltpu.VMEM((XA_HEADS, tm, M), F32), pltpu.VMEM((tm, D), BF16)],
        compiler_params=_cparams(("parallel",)),
        name="xattn",
    )(h, g, wq, kv, kv, wo)


def kernel(x, mem, ffn1_norm, ffn1_w_gate, ffn1_w_up, ffn1_w_down, mix_norm, w_in, sgu_norm, sgu_w_s, sgu_b, ret_gn, q_norm, w_uq, kv_norm, w_ukv, w_out, xa_norm, mem_norm, xa_wq, xa_wkv, xa_wo, ffn2_norm, ffn2_w_gate, ffn2_w_up, ffn2_w_down, final_norm):
    B, S, D = x.shape
    M = mem.shape[1]
    L = w_in.shape[0]
    T = B * S
    a_width = sgu_norm.shape[-1]
    r_width = ret_gn.shape[-1]
    q_lora = q_norm.shape[-1]
    kv_lora = kv_norm.shape[-1]
    assert a_width == A_GROUPS * CHUNK and r_width == R_HEADS * CHUNK and S % CHUNK == 0
    ret_col0 = 2 * a_width
    mla_col0 = ret_col0 + 4 * r_width
    kr_col0 = mla_col0 + q_lora + kv_lora
    assert w_in.shape[-1] == kr_col0 + MLA_ROPE

    bf = lambda w: w.astype(BF16)
    row = lambda g: g.reshape(g.shape[0], 1, g.shape[1])

    n_in = kr_col0 + MLA_ROPE
    n_in_pad = -(-n_in // COL_CHUNK) * COL_CHUNK
    w_in_x = bf(jnp.pad(w_in, ((0, 0), (0, 0), (0, n_in_pad - n_in))))
    wq4 = w_uq.reshape(L, q_lora, MLA_HEADS, MLA_NOPE + MLA_ROPE)
    wq_x = bf(jnp.concatenate([wq4, _swap_halves(wq4[..., MLA_NOPE:])], axis=-1)).reshape(L, q_lora, -1)
    wkv4 = w_ukv.reshape(L, kv_lora, MLA_HEADS, MLA_NOPE + MLA_V)
    wkv_x = bf(jnp.concatenate([wkv4[..., :MLA_NOPE].reshape(L, kv_lora, -1),
                                wkv4[..., MLA_NOPE:].reshape(L, kv_lora, -1)], axis=-1))
    sgu_bt = jnp.swapaxes(sgu_b, 1, 2)
    ret_tabs = _retention_tables(S)
    mla_rt = _mla_rope_table(S)
    w1g, w1u, w1d = bf(ffn1_w_gate), ffn1_w_up, ffn1_w_down
    w2g, w2u, w2d = bf(ffn2_w_gate), ffn2_w_up, ffn2_w_down
    w_out_b, xa_wq_b, xa_wkv_b, xa_wo_b = w_out, bf(xa_wq), xa_wkv, bf(xa_wo)
    fin = final_norm.reshape(1, D)

    h = x.reshape(T, D)
    mem2 = mem.reshape(B * M, D)
    for l in range(L):
        h = _ffn(h, row(ffn1_norm), w1g, w1u, w1d, l, fin, False)
        proj = _norm_mm(h, row(mix_norm), w_in_x, l, F32)
        y_a = _sgu(proj, row(sgu_norm), sgu_w_s, sgu_bt, l, a_width)
        y_r = _retention(proj, *ret_tabs, row(ret_gn), l, B, S, r_width, ret_col0)
        q, k, v = _mla_proj(proj, row(q_norm), row(kv_norm), wq_x, wkv_x, mla_rt, l, S, q_lora, kv_lora, mla_col0)
        y_c = _flash(q, k, v, B, S)
        h = _mm_res([y_a, y_r, y_c], w_out_b, l, h)
        kvm = _norm_mm(mem2, row(mem_norm), xa_wkv_b, l, BF16)
        h = _xattn(h, row(xa_norm), xa_wq_b, kvm, xa_wo_b, l, S, M)
        h = _ffn(h, row(ffn2_norm), w2g, w2u, w2d, l, fin, l == L - 1)
    return h.reshape(B, S, D)
```
